```python
import jax, jax.numpy as jnp
from jax import lax
import numpy as np

D_MODEL = 1024
BATCH = 16
SEQ = 4096
DEPTH = 4

N_MIXERS = 4
EPS = 1e-6
D_FF = 4 * D_MODEL

A_CHUNK = 128
A_GROUPS = 8
A_WIDTH = D_MODEL
A_GDIM = A_WIDTH // A_GROUPS

B_WIDTH = D_MODEL
B_KSIZE = 3

C_WINDOWS = (2, 4, 8, 16)
C_GROUPS = len(C_WINDOWS)
C_GDIM = D_MODEL // C_GROUPS

D_HEADS = 16
D_LAT = 128
D_VDIM = D_MODEL // D_HEADS
D_IDX_HEADS = 8
D_IDX_DIM = 64
D_TOPK_MAX = 256
D_QBLOCK = 128
D_SPLITS = (D_HEADS * D_LAT,
            D_HEADS * D_LAT + D_LAT,
            D_HEADS * D_LAT + D_LAT + D_IDX_HEADS * D_IDX_DIM,
            D_HEADS * D_LAT + D_LAT + D_IDX_HEADS * D_IDX_DIM + D_IDX_DIM)
D_IN_COLS = D_SPLITS[-1] + D_IDX_HEADS

kernel_name = "hybrid_interleaved_sgu_conv_pool_dsa"


def _type_counts():
    return [sum(1 for i in range(DEPTH) if i % N_MIXERS == m) for m in range(N_MIXERS)]


def rmsnorm(x, g):
    xf = x.astype(jnp.float32)
    y = xf * lax.rsqrt(jnp.mean(xf * xf, axis=-1, keepdims=True) + EPS)
    return (y * g.astype(jnp.float32)).astype(x.dtype)


def mixer_chunked_sgu(h, w_in, v_g, w_s, b_s, w_out):
    bsz, L, _ = h.shape
    z = jax.nn.gelu(h @ w_in)
    u, v = jnp.split(z, 2, axis=-1)
    v = rmsnorm(v, v_g)
    n = L // A_CHUNK
    v = v.reshape(bsz, n, A_CHUNK, A_GROUPS, A_GDIM)
    causal = jnp.tril(jnp.ones((A_CHUNK, A_CHUNK), dtype=bool))
    ws = jnp.where(causal[None], w_s, jnp.zeros_like(w_s)).astype(v.dtype)
    s = jnp.einsum('gts,bnsgc->bntgc', ws, v) + b_s.T.astype(v.dtype)[None, None, :, :, None]
    s = s.reshape(bsz, L, A_WIDTH)
    return (u * s) @ w_out


def mixer_short_conv(h, w_in, conv_w, w_out):
    bg, cg, hv = jnp.split(h @ w_in, 3, axis=-1)
    z = cg * hv
    y = lax.conv_general_dilated(z, conv_w.astype(z.dtype), window_strides=(1,),
                                 padding=[(B_KSIZE - 1, 0)],
                                 dimension_numbers=('NWC', 'WIO', 'NWC'),
                                 feature_group_count=B_WIDTH)
    return (bg * y) @ w_out


def mixer_multiscale_pool(h, w_in, w_grp, scale):
    bsz, L, _ = h.shape
    z = h @ w_in
    zf = z.astype(jnp.float32).reshape(bsz, L, C_GROUPS, C_GDIM)
    cs = jnp.cumsum(zf, axis=1)
    cs = jnp.concatenate([jnp.zeros_like(cs[:, :1]), cs], axis=1)
    t = jnp.arange(L)
    pooled = []
    for g, w in enumerate(C_WINDOWS):
        lo = jnp.maximum(t + 1 - w, 0)
        seg = cs[:, 1:, g] - cs[:, lo, g]
        cnt = (t + 1 - lo).astype(jnp.float32)
        pooled.append(seg / cnt[None, :, None])
    pooled = jnp.stack(pooled, axis=2) - zf
    y = jnp.einsum('blgc,gcd->blgd', pooled.astype(z.dtype), w_grp)
    return y.reshape(bsz, L, D_MODEL) * scale


def mixer_dsa(h, w_in, kv_g, w_uv, w_out):
    bsz, L, _ = h.shape
    topk = min(D_TOPK_MAX, L // 4)
    q, c, qi, ki, wi = jnp.split(h @ w_in, list(D_SPLITS), axis=-1)
    q = q.reshape(bsz, L, D_HEADS, D_LAT)
    c = rmsnorm(c, kv_g)
    qi = qi.reshape(bsz, L, D_IDX_HEADS, D_IDX_DIM)
    wi = wi * (D_IDX_HEADS ** -0.5 * D_IDX_DIM ** -0.5)
    n_blk = L // D_QBLOCK
    key_pos = jnp.arange(L)

    def to_blocks(a):
        return a.reshape(bsz, n_blk, D_QBLOCK, *a.shape[2:]).swapaxes(0, 1)

    def block(args):
        qb, qib, wib, blk = args
        qpos = blk * D_QBLOCK + jnp.arange(D_QBLOCK)
        logits = jnp.einsum('bthd,bsd->bths', qib, ki)
        score = jnp.einsum('bths,bth->bts', jax.nn.relu(logits).astype(jnp.float32),
                           wib.astype(jnp.float32))
        causal = key_pos[None, :] <= qpos[:, None]
        score = jnp.where(causal[None], score, -jnp.inf)
        _, idx = lax.top_k(score, topk)
        kv_sel = jax.vmap(lambda cb, ib: cb[ib])(c, idx)
        valid = idx <= qpos[None, :, None]
        att = jnp.einsum('bthd,btkd->bthk', qb, kv_sel).astype(jnp.float32) * (D_LAT ** -0.5)
        att = jnp.where(valid[:, :, None, :], att, -jnp.inf)
        p = jax.nn.softmax(att, axis=-1).astype(c.dtype)
        return jnp.einsum('bthk,btkd->bthd', p, kv_sel)

    o = lax.map(block, (to_blocks(q), to_blocks(qi), to_blocks(wi), jnp.arange(n_blk)))
    o = o.swapaxes(0, 1).reshape(bsz, L, D_HEADS, D_LAT)
    o = jnp.einsum('blhc,hcv->blhv', o, w_uv).reshape(bsz, L, D_HEADS * D_VDIM)
    return o @ w_out


def channel_mlp(h, w1, w2):
    a = jax.nn.relu(h @ w1)
    return (a * a) @ w2


def setup_inputs(seed: int = 0) -> dict:
    key = jax.random.key(seed)
    ks = iter(jax.random.split(key, 32))
    nA, nB, nC, nD = _type_counts()
    f32 = jnp.float32

    def nrm(shape, scale):
        return jax.random.normal(next(ks), shape, f32) * scale

    def gain(shape):
        return 1.0 + 0.02 * jax.random.normal(next(ks), shape, f32)

    return {
        "x": jax.random.normal(next(ks), (BATCH, SEQ, D_MODEL), f32),
        "norm_mix_g": gain((DEPTH, D_MODEL)),
        "norm_mlp_g": gain((DEPTH, D_MODEL)),
        "final_g": gain((D_MODEL,)),
        "a_w_in": nrm((nA, D_MODEL, 2 * A_WIDTH), D_MODEL ** -0.5),
        "a_v_g": gain((nA, A_WIDTH)),
        "a_w_s": nrm((nA, A_GROUPS, A_CHUNK, A_CHUNK), A_CHUNK ** -0.5),
        "a_b_s": gain((nA, A_GROUPS, A_CHUNK)),
        "a_w_out": nrm((nA, A_WIDTH, D_MODEL), A_WIDTH ** -0.5),
        "b_w_in": nrm((nB, D_MODEL, 3 * B_WIDTH), D_MODEL ** -0.5),
        "b_conv_w": nrm((nB, B_KSIZE, 1, B_WIDTH), B_KSIZE ** -0.5),
        "b_w_out": nrm((nB, B_WIDTH, D_MODEL), B_WIDTH ** -0.5),
        "c_w_in": nrm((nC, D_MODEL, D_MODEL), D_MODEL ** -0.5),
        "c_w_grp": nrm((nC, C_GROUPS, C_GDIM, C_GDIM), C_GDIM ** -0.5),
        "c_scale": gain((nC, D_MODEL)),
        "d_w_in": nrm((nD, D_MODEL, D_IN_COLS), D_MODEL ** -0.5),
        "d_kv_g": gain((nD, D_LAT)),
        "d_w_uv": nrm((nD, D_HEADS, D_LAT, D_VDIM), D_LAT ** -0.5),
        "d_w_out": nrm((nD, D_HEADS * D_VDIM, D_MODEL), (D_HEADS * D_VDIM) ** -0.5),
        "mlp_w1": nrm((DEPTH, D_MODEL, D_FF), D_MODEL ** -0.5),
        "mlp_w2": nrm((DEPTH, D_FF, D_MODEL), D_FF ** -0.5),
    }


def reference(x, norm_mix_g, norm_mlp_g, final_g, a_w_in, a_v_g, a_w_s, a_b_s, a_w_out,
              b_w_in, b_conv_w, b_w_out, c_w_in, c_w_grp, c_scale,
              d_w_in, d_kv_g, d_w_uv, d_w_out, mlp_w1, mlp_w2):
    for i in range(DEPTH):
        m, j = i % N_MIXERS, i // N_MIXERS
        h = rmsnorm(x, norm_mix_g[i])
        if m == 0:
            y = mixer_chunked_sgu(h, a_w_in[j], a_v_g[j], a_w_s[j], a_b_s[j], a_w_out[j])
        elif m == 1:
            y = mixer_short_conv(h, b_w_in[j], b_conv_w[j], b_w_out[j])
        elif m == 2:
            y = mixer_multiscale_pool(h, c_w_in[j], c_w_grp[j], c_scale[j])
        else:
            y = mixer_dsa(h, d_w_in[j], d_kv_g[j], d_w_uv[j], d_w_out[j])
        x = x + y
        x = x + channel_mlp(rmsnorm(x, norm_mlp_g[i]), mlp_w1[i], mlp_w2[i])
    return rmsnorm(x, final_g)
```

```python
import functools

import jax
import jax.numpy as jnp
from jax import lax
from jax.experimental import pallas as pl
from jax.experimental.pallas import tpu as pltpu

EPS = 1e-6
LANES = 128
VMEM_LIMIT_BYTES = 56 * 1024 * 1024

A_CHUNK = 128
A_GROUPS = 8
B_KSIZE = 3
C_WINDOWS = (2, 4, 8, 16)
D_HEADS = 16
D_LAT = 128
D_VDIM = 64
D_IDX_HEADS = 8
D_IDX_DIM = 64
D_TOPK_MAX = 256

INT_MIN = -(2 ** 31)
NEG_BIAS = -1e30
M_INIT = -1e29
LOG2E = 1.4426950408889634

TOKEN_TILE = 512
DSA_Q_TILE = 128
DSA_K_TILE = 256

f32 = jnp.float32
bf16 = jnp.bfloat16
i32 = jnp.int32


def _dot(a, b):
    return jnp.dot(a, b, preferred_element_type=f32)


def _dot_nt(a, b):
    return lax.dot_general(a, b, (((1,), (1,)), ((), ())), preferred_element_type=f32)


def _rms(x, g):
    ms = jnp.mean(x * x, axis=-1, keepdims=True)
    return x * lax.rsqrt(ms + EPS) * g


def _params(*sem):
    return pltpu.CompilerParams(dimension_semantics=sem, vmem_limit_bytes=VMEM_LIMIT_BYTES)


def _full(shape):
    nd = len(shape)
    return pl.BlockSpec(shape, lambda *_: (0,) * nd)


def _mlp_kernel(x_ref, g_ref, w1_ref, w2_ref, fg_ref, o_ref, *, ff_chunk, apply_final):
    x = x_ref[...]
    h = _rms(x, g_ref[...]).astype(bf16)
    acc = x
    for c in range(w1_ref.shape[1] // ff_chunk):
        a = jnp.maximum(_dot(h, w1_ref[:, c * ff_chunk:(c + 1) * ff_chunk]), 0.0)
        acc = acc + _dot((a * a).astype(bf16), w2_ref[c * ff_chunk:(c + 1) * ff_chunk, :])
    if apply_final:
        acc = _rms(acc, fg_ref[...])
    o_ref[...] = acc


def _mlp(x, g, w1, w2, final_g, apply_final):
    n, d = x.shape
    dff = w1.shape[1]
    tm = min(TOKEN_TILE, n)
    row = pl.BlockSpec((tm, d), lambda i: (i, 0))
    return pl.pallas_call(
        functools.partial(_mlp_kernel, ff_chunk=min(1024, dff), apply_final=apply_final),
        grid=(n // tm,),
        in_specs=[row, _full((1, d)), _full((d, dff)), _full((dff, d)), _full((1, d))],
        out_specs=row,
        out_shape=jax.ShapeDtypeStruct((n, d), f32),
        compiler_params=_params("parallel"),
    )(x, g.reshape(1, d), w1.astype(bf16), w2.astype(bf16), final_g.reshape(1, d))


def _sgu_kernel(x_ref, g_ref, win_ref, vg_ref, ws_ref, bst_ref, wout_ref, o_ref, s_ref):
    tm, _ = x_ref.shape
    width = wout_ref.shape[0]
    gdim = width // A_GROUPS
    x = x_ref[...]
    h = _rms(x, g_ref[...]).astype(bf16)
    u = jax.nn.gelu(_dot(h, win_ref[:, :width]))
    v = jax.nn.gelu(_dot(h, win_ref[:, width:]))
    v = _rms(v, vg_ref[...]).astype(bf16)
    r = lax.broadcasted_iota(i32, (A_CHUNK, A_CHUNK), 0)
    c = lax.broadcasted_iota(i32, (A_CHUNK, A_CHUNK), 1)
    causal = c <= r
    bst = bst_ref[...]
    for gi in range(A_GROUPS):
        ws = jnp.where(causal, ws_ref[gi], 0.0).astype(bf16)
        bias = jnp.broadcast_to(bst[:, gi:gi + 1], (A_CHUNK, gdim))
        for n in range(tm // A_CHUNK):
            vb = v[n * A_CHUNK:(n + 1) * A_CHUNK, gi * gdim:(gi + 1) * gdim]
            s_ref[n * A_CHUNK:(n + 1) * A_CHUNK, gi * gdim:(gi + 1) * gdim] = _dot(ws, vb) + bias
    o_ref[...] = x + _dot((u * s_ref[...]).astype(bf16), wout_ref[...])


def _mixer_sgu(x, g, w_in, v_g, w_s, b_s, w_out):
    n, d = x.shape
    width = w_out.shape[0]
    tm = min(TOKEN_TILE, n)
    row = pl.BlockSpec((tm, d), lambda i: (i, 0))
    return pl.pallas_call(
        _sgu_kernel,
        grid=(n // tm,),
        in_specs=[row, _full((1, d)), _full((d, 2 * width)), _full((1, width)),
                  _full(w_s.shape), _full((A_CHUNK, A_GROUPS)), _full((width, d))],
        out_specs=row,
        out_shape=jax.ShapeDtypeStruct((n, d), f32),
        scratch_shapes=[pltpu.VMEM((tm, width), f32)],
        compiler_params=_params("parallel"),
    )(x, g.reshape(1, d), w_in.astype(bf16), v_g.reshape(1, width), w_s, b_s.T, w_out.astype(bf16))


CONV_HALO = 8


def _conv_kernel(x_ref, g_ref, win_ref, cw_ref, wout_ref, o_ref, z_ref):
    tm = x_ref.shape[1]
    width = wout_ref.shape[0]

    @pl.when(pl.program_id(1) == 0)
    def _():
        z_ref[0:CONV_HALO, :] = jnp.zeros((CONV_HALO, width), f32)

    x = x_ref[0]
    h = _rms(x, g_ref[...]).astype(bf16)
    bg = _dot(h, win_ref[:, :width])
    z = _dot(h, win_ref[:, width:2 * width]) * _dot(h, win_ref[:, 2 * width:])
    z_ref[CONV_HALO:CONV_HALO + tm, :] = z
    y = cw_ref[B_KSIZE - 1:B_KSIZE, :] * z
    for k in range(B_KSIZE - 1):
        shift = B_KSIZE - 1 - k
        y = y + cw_ref[k:k + 1, :] * z_ref[CONV_HALO - shift:CONV_HALO - shift + tm, :]
    z_ref[0:CONV_HALO, :] = z[tm - CONV_HALO:, :]
    o_ref[0] = x + _dot((bg * y).astype(bf16), wout_ref[...])


def _mixer_conv(x, g, w_in, conv_w, w_out):
    b, l, d = x.shape
    width = w_out.shape[0]
    tm = min(TOKEN_TILE, l)
    row = pl.BlockSpec((1, tm, d), lambda i, j: (i, j, 0))
    return pl.pallas_call(
        _conv_kernel,
        grid=(b, l // tm),
        in_specs=[row, _full((1, d)), _full((d, 3 * width)), _full((B_KSIZE, width)), _full((width, d))],
        out_specs=row,
        out_shape=jax.ShapeDtypeStruct((b, l, d), f32),
        scratch_shapes=[pltpu.VMEM((CONV_HALO + tm, width), f32)],
        compiler_params=_params("arbitrary", "arbitrary"),
    )(x, g.reshape(1, d), w_in.astype(bf16), conv_w.reshape(B_KSIZE, width), w_out.astype(bf16))


POOL_HALO = 16
POOL_PAD = 8


def _pool_kernel(x_ref, g_ref, win_ref, wgrp_ref, sc_ref, o_ref, carry_ref, pa_ref, pb_ref, y_ref):
    tm = x_ref.shape[1]
    d = x_ref.shape[2]
    gdim = d // len(C_WINDOWS)
    n_ext = tm + POOL_HALO
    j = pl.program_id(1)

    @pl.when(j == 0)
    def _():
        carry_ref[...] = jnp.zeros(carry_ref.shape, f32)

    x = x_ref[0]
    h = _rms(x, g_ref[...]).astype(bf16)
    z = _dot(h, win_ref[...])
    pa_ref[0:POOL_PAD, :] = jnp.zeros((POOL_PAD, gdim), f32)
    pb_ref[0:POOL_PAD, :] = jnp.zeros((POOL_PAD, gdim), f32)
    pos = j * tm + lax.broadcasted_iota(i32, (tm, 1), 0) + 1
    for gi, w in enumerate(C_WINDOWS):
        zg = z[:, gi * gdim:(gi + 1) * gdim]
        src, dst = pa_ref, pb_ref
        src[POOL_PAD:POOL_PAD + POOL_HALO, :] = carry_ref[:, gi * gdim:(gi + 1) * gdim]
        src[POOL_PAD + POOL_HALO:, :] = zg
        shift = 1
        while True:
            acc = src[POOL_PAD:, :] + src[POOL_PAD - shift:POOL_PAD - shift + n_ext, :]
            shift *= 2
            if shift == w:
                break
            dst[POOL_PAD:, :] = acc
            src, dst = dst, src
        cnt = jnp.minimum(pos, w).astype(f32)
        pooled = acc[POOL_HALO:, :] / cnt - zg
        y_ref[:, gi * gdim:(gi + 1) * gdim] = _dot(pooled.astype(bf16), wgrp_ref[gi])
    carry_ref[...] = z[tm - POOL_HALO:, :]
    o_ref[0] = x + y_ref[...] * sc_ref[...]


def _mixer_pool(x, g, w_in, w_grp, scale):
    b, l, d = x.shape
    gdim = d // len(C_WINDOWS)
    tm = min(TOKEN_TILE, l)
    row = pl.BlockSpec((1, tm, d), lambda i, j: (i, j, 0))
    ext = POOL_PAD + POOL_HALO + tm
    return pl.pallas_call(
        _pool_kernel,
        grid=(b, l // tm),
        in_specs=[row, _full((1, d)), _full((d, d)), _full(w_grp.shape), _full((1, d))],
        out_specs=row,
        out_shape=jax.ShapeDtypeStruct((b, l, d), f32),
        scratch_shapes=[pltpu.VMEM((POOL_HALO, d), f32), pltpu.VMEM((ext, gdim), f32),
                        pltpu.VMEM((ext, gdim), f32), pltpu.VMEM((tm, d), f32)],
        compiler_params=_params("arbitrary", "arbitrary"),
    )(x, g.reshape(1, d), w_in.astype(bf16), w_grp.astype(bf16), scale.reshape(1, d))


def _dsa_proj_kernel(x_ref, g_ref, wq_ref, wc_ref, wqi_ref, wkw_ref, kvg_ref,
                     q_ref, c_ref, qi_ref, ki_ref, wi_ref):
    tm = x_ref.shape[1]
    h = _rms(x_ref[0], g_ref[...]).astype(bf16)
    q = _dot(h, wq_ref[...])
    for hd in range(D_HEADS):
        q_ref[0, hd] = q[:, hd * D_LAT:(hd + 1) * D_LAT].astype(bf16)
    c = _rms(_dot(h, wc_ref[...]), kvg_ref[...]).astype(bf16)
    c_ref[0, :, :D_LAT] = c
    c_ref[0, :, D_LAT:] = jnp.ones((tm, D_LAT), bf16)
    qi = _dot(h, wqi_ref[...])
    for hd in range(D_IDX_HEADS):
        qi_ref[0, hd] = qi[:, hd * D_IDX_DIM:(hd + 1) * D_IDX_DIM].astype(bf16)
    kw = _dot(h, wkw_ref[...])
    ki_ref[0] = kw[:, :D_IDX_DIM].astype(bf16)
    wi_ref[0] = kw[:, D_IDX_DIM:D_IDX_DIM + D_IDX_HEADS] * (D_IDX_HEADS ** -0.5 * D_IDX_DIM ** -0.5)


def _dsa_proj(x, g, w_in, kv_g):
    b, l, d = x.shape
    nq = D_HEADS * D_LAT
    nqi = D_IDX_HEADS * D_IDX_DIM
    o_c, o_qi, o_ki = nq, nq + D_LAT, nq + D_LAT + nqi
    w_in = w_in.astype(bf16)
    w_kw = jnp.pad(w_in[:, o_ki:], ((0, 0), (0, LANES - (D_IDX_DIM + D_IDX_HEADS))))
    tm = min(TOKEN_TILE, l)
    row = pl.BlockSpec((1, tm, d), lambda i, j: (i, j, 0))
    return pl.pallas_call(
        _dsa_proj_kernel,
        grid=(b, l // tm),
        in_specs=[row, _full((1, d)), _full((d, nq)), _full((d, D_LAT)), _full((d, nqi)),
                  _full((d, LANES)), _full((1, D_LAT))],
        out_specs=[pl.BlockSpec((1, D_HEADS, tm, D_LAT), lambda i, j: (i, 0, j, 0)),
                   pl.BlockSpec((1, tm, 2 * D_LAT), lambda i, j: (i, j, 0)),
                   pl.BlockSpec((1, D_IDX_HEADS, tm, D_IDX_DIM), lambda i, j: (i, 0, j, 0)),
                   pl.BlockSpec((1, tm, D_IDX_DIM), lambda i, j: (i, j, 0)),
                   pl.BlockSpec((1, tm, D_IDX_HEADS), lambda i, j: (i, j, 0))],
        out_shape=[jax.ShapeDtypeStruct((b, D_HEADS, l, D_LAT), bf16),
                   jax.ShapeDtypeStruct((b, l, 2 * D_LAT), bf16),
                   jax.ShapeDtypeStruct((b, D_IDX_HEADS, l, D_IDX_DIM), bf16),
                   jax.ShapeDtypeStruct((b, l, D_IDX_DIM), bf16),
                   jax.ShapeDtypeStruct((b, l, D_IDX_HEADS), f32)],
        compiler_params=_params("parallel", "parallel"),
    )(x, g.reshape(1, d), w_in[:, :o_c], w_in[:, o_c:o_qi], w_in[:, o_qi:o_ki], w_kw,
      kv_g.reshape(1, D_LAT))


def _lane_tile(a, n):
    return a if n == LANES else jnp.concatenate([a] * (n // LANES), axis=1)


def _dsa_attn_kernel(q_ref, qi_ref, wi_ref, c_ref, ki_ref, o_ref,
                     key_ref, wb_ref, lg_ref, p_ref, m_ref, alpha_ref, acc_ref, *, topk):
    tq = q_ref.shape[2]
    tk = DSA_K_TILE
    t0 = pl.program_id(1) * tq
    n_kt = (t0 + tq + tk - 1) // tk
    kf = float(topk)

    wi = wi_ref[0]
    for hd in range(D_IDX_HEADS):
        wb_ref[hd * tq:(hd + 1) * tq, :] = jnp.broadcast_to(wi[:, hd:hd + 1], (tq, LANES))
    qi = qi_ref[0].reshape(D_IDX_HEADS * tq, D_IDX_DIM)
    rows = t0 + lax.broadcasted_iota(i32, (tq, tk), 0)
    col_iota = lax.broadcasted_iota(i32, (tq, tk), 1)

    def score_body(j, carry):
        k0 = pl.multiple_of(j * tk, tk)
        lg = _dot_nt(qi, ki_ref[0, pl.ds(k0, tk), :])
        s = None
        for hd in range(D_IDX_HEADS):
            term = jnp.maximum(lg[hd * tq:(hd + 1) * tq, :], 0.0) * _lane_tile(wb_ref[hd * tq:(hd + 1) * tq, :], tk)
            s = term if s is None else s + term
        s = jnp.where(s == 0.0, 0.0, s)
        bits = pltpu.bitcast(s, i32)
        key = bits ^ ((bits >> 31) & 0x7FFFFFFF)
        key_ref[:, pl.ds(k0, tk)] = jnp.where(k0 + col_iota <= rows, key, INT_MIN)
        return carry

    lax.fori_loop(0, n_kt, score_body, 0)

    def count_ge(cand):
        cb = jnp.broadcast_to(cand, (tq, LANES))

        def body(j, acc):
            k0 = pl.multiple_of(j * tk, tk)
            blk = key_ref[:, pl.ds(k0, tk)]
            for cc in range(tk // LANES):
                acc = acc + jnp.where(blk[:, cc * LANES:(cc + 1) * LANES] >= cb, 1.0, 0.0)
            return acc

        acc = lax.fori_loop(0, n_kt, body, jnp.zeros((tq, LANES), f32))
        return jnp.sum(acc, axis=1, keepdims=True)

    cnt0 = count_ge(jnp.zeros((tq, 1), i32))
    thr = jnp.where(cnt0 >= kf, 0, INT_MIN).astype(i32)
    cnt_thr = jnp.where(cnt0 >= kf, cnt0, 0.0)

    def bit_body(i, carry):
        thr, cnt_thr = carry
        cand = thr | lax.shift_left(jnp.int32(1), 30 - i)
        cnt = count_ge(cand)
        ok = cnt >= kf
        return jnp.where(ok, cand, thr), jnp.where(ok, cnt, cnt_thr)

    thr, cnt_thr = lax.fori_loop(0, 31, bit_body, (thr, cnt_thr))
    thr = jnp.maximum(thr, INT_MIN + 1)

    @pl.when(jnp.max(cnt_thr) > kf)
    def _():
        keep_n = kf - count_ge(thr + 1)
        tb = jnp.broadcast_to(thr, (tq, LANES))
        tri = (lax.broadcasted_iota(i32, (LANES, LANES), 0)
               <= lax.broadcasted_iota(i32, (LANES, LANES), 1)).astype(bf16)

        def tie_body(j, run):
            k0 = pl.multiple_of(j * LANES, LANES)
            blk = key_ref[:, pl.ds(k0, LANES)]
            tie = jnp.where(blk == tb, 1.0, 0.0)
            rank = run + _dot(tie.astype(bf16), tri)
            drop = (tie * jnp.where(rank > keep_n, 1.0, 0.0)) > 0.5
            key_ref[:, pl.ds(k0, LANES)] = jnp.where(drop, INT_MIN, blk)
            return run + jnp.sum(tie, axis=1, keepdims=True)

        lax.fori_loop(0, n_kt * (tk // LANES), tie_body, jnp.zeros((tq, 1), f32))

    m_ref[...] = jnp.full(m_ref.shape, M_INIT, f32)
    acc_ref[...] = jnp.zeros(acc_ref.shape, f32)
    q = q_ref[0].reshape(D_HEADS * tq, D_LAT)
    tb = jnp.broadcast_to(thr, (tq, LANES))
    scale = D_LAT ** -0.5 * LOG2E

    def att_body(j, carry):
        k0 = pl.multiple_of(j * tk, tk)
        ce = c_ref[0, pl.ds(k0, tk), :]
        bias = jnp.where(key_ref[:, pl.ds(k0, tk)] >= _lane_tile(tb, tk), 0.0, NEG_BIAS)
        lg_ref[...] = _dot_nt(q, ce[:, :D_LAT])
        for hd in range(D_HEADS):
            rs = slice(hd * tq, (hd + 1) * tq)
            lh = lg_ref[rs, :] + bias
            m_prev = m_ref[rs, :]
            m_new = jnp.maximum(m_prev, jnp.max(lh, axis=1, keepdims=True))
            alpha_ref[rs, :] = jnp.exp2((m_prev - m_new) * scale)
            p_ref[rs, :] = jnp.exp2((lh - _lane_tile(m_new, tk)) * scale).astype(bf16)
            m_ref[rs, :] = m_new
        alpha = alpha_ref[...]
        acc_ref[...] = acc_ref[...] * jnp.concatenate([alpha, alpha], axis=1) + _dot(p_ref[...], ce)
        return carry

    lax.fori_loop(0, n_kt, att_body, 0)

    for hd in range(D_HEADS):
        rs = slice(hd * tq, (hd + 1) * tq)
        o_ref[0, :, hd * D_LAT:(hd + 1) * D_LAT] = (acc_ref[rs, :D_LAT] / acc_ref[rs, D_LAT:]).astype(bf16)


def _dsa_attn(q, qi, wi, c_ext, ki, topk):
    b, _, l, _ = q.shape
    tq = min(DSA_Q_TILE, l)
    assert l % DSA_K_TILE == 0 and l % tq == 0
    return pl.pallas_call(
        functools.partial(_dsa_attn_kernel, topk=topk),
        grid=(b, l // tq),
        in_specs=[pl.BlockSpec((1, D_HEADS, tq, D_LAT), lambda i, j: (i, 0, j, 0)),
                  pl.BlockSpec((1, D_IDX_HEADS, tq, D_IDX_DIM), lambda i, j: (i, 0, j, 0)),
                  pl.BlockSpec((1, tq, D_IDX_HEADS), lambda i, j: (i, j, 0)),
                  pl.BlockSpec((1, l, 2 * D_LAT), lambda i, j: (i, 0, 0)),
                  pl.BlockSpec((1, l, D_IDX_DIM), lambda i, j: (i, 0, 0))],
        out_specs=pl.BlockSpec((1, tq, D_HEADS * D_LAT), lambda i, j: (i, j, 0)),
        out_shape=jax.ShapeDtypeStruct((b, l, D_HEADS * D_LAT), bf16),
        scratch_shapes=[pltpu.VMEM((tq, l), i32),
                        pltpu.VMEM((D_IDX_HEADS * tq, LANES), f32),
                        pltpu.VMEM((D_HEADS * tq, DSA_K_TILE), f32),
                        pltpu.VMEM((D_HEADS * tq, DSA_K_TILE), bf16),
                        pltpu.VMEM((D_HEADS * tq, LANES), f32),
                        pltpu.VMEM((D_HEADS * tq, LANES), f32),
                        pltpu.VMEM((D_HEADS * tq, 2 * D_LAT), f32)],
        compiler_params=_params("parallel", "arbitrary"),
    )(q, qi, wi, c_ext, ki)


def _dsa_out_kernel(x_ref, o_ref, wuv_ref, wout_ref, y_ref, ov_ref):
    for hd in range(D_HEADS):
        ov_ref[:, hd * D_VDIM:(hd + 1) * D_VDIM] = _dot(o_ref[:, hd * D_LAT:(hd + 1) * D_LAT], wuv_ref[hd])
    y_ref[...] = x_ref[...] + _dot(ov_ref[...].astype(bf16), wout_ref[...])


def _dsa_out(x, o, w_uv, w_out):
    n, d = x.shape
    tm = min(TOKEN_TILE, n)
    return pl.pallas_call(
        _dsa_out_kernel,
        grid=(n // tm,),
        in_specs=[pl.BlockSpec((tm, d), lambda i: (i, 0)),
                  pl.BlockSpec((tm, D_HEADS * D_LAT), lambda i: (i, 0)),
                  _full(w_uv.shape), _full(w_out.shape)],
        out_specs=pl.BlockSpec((tm, d), lambda i: (i, 0)),
        out_shape=jax.ShapeDtypeStruct((n, d), f32),
        scratch_shapes=[pltpu.VMEM((tm, D_HEADS * D_VDIM), f32)],
        compiler_params=_params("parallel"),
    )(x, o, w_uv.astype(bf16), w_out.astype(bf16))


def _mixer_dsa(x, g, w_in, kv_g, w_uv, w_out):
    b, l, d = x.shape
    topk = min(D_TOPK_MAX, l // 4)
    q, c_ext, qi, ki, wi = _dsa_proj(x, g, w_in, kv_g)
    o = _dsa_attn(q, qi, wi, c_ext, ki, topk)
    return _dsa_out(x.reshape(b * l, d), o.reshape(b * l, D_HEADS * D_LAT), w_uv, w_out).reshape(b, l, d)


@jax.jit
def _trunk(x, norm_mix_g, norm_mlp_g, final_g, a_w_in, a_v_g, a_w_s, a_b_s, a_w_out, b_w_in, b_conv_w,
           b_w_out, c_w_in, c_w_grp, c_scale, d_w_in, d_kv_g, d_w_uv, d_w_out, mlp_w1, mlp_w2):
    b, l, d = x.shape
    depth = norm_mix_g.shape[0]
    n_mixers = 4
    for i in range(depth):
        m, j = i % n_mixers, i // n_mixers
        g = norm_mix_g[i]
        if m == 0:
            x = _mixer_sgu(x.reshape(b * l, d), g, a_w_in[j], a_v_g[j], a_w_s[j], a_b_s[j],
                           a_w_out[j]).reshape(b, l, d)
        elif m == 1:
            x = _mixer_conv(x, g, b_w_in[j], b_conv_w[j], b_w_out[j])
        elif m == 2:
            x = _mixer_pool(x, g, c_w_in[j], c_w_grp[j], c_scale[j])
        else:
            x = _mixer_dsa(x, g, d_w_in[j], d_kv_g[j], d_w_uv[j], d_w_out[j])
        x = _mlp(x.reshape(b * l, d), norm_mlp_g[i], mlp_w1[i], mlp_w2[i], final_g,
                 apply_final=(i == depth - 1)).reshape(b, l, d)
    return x


def kernel(x, norm_mix_g, norm_mlp_g, final_g, a_w_in, a_v_g, a_w_s, a_b_s, a_w_out, b_w_in, b_conv_w,
           b_w_out, c_w_in, c_w_grp, c_scale, d_w_in, d_kv_g, d_w_uv, d_w_out, mlp_w1, mlp_w2):
    return _trunk(x, norm_mix_g, norm_mlp_g, final_g, a_w_in, a_v_g, a_w_s, a_b_s, a_w_out, b_w_in,
                  b_conv_w, b_w_out, c_w_in, c_w_grp, c_scale, d_w_in, d_kv_g, d_w_uv, d_w_out,
                  mlp_w1, mlp_w2)
```

```python
import functools

import jax
import jax.numpy as jnp
from jax import lax
from jax.experimental import pallas as pl
from jax.experimental.pallas import tpu as pltpu

EPS = 1e-6
LANES = 128
VMEM_LIMIT_BYTES = 56 * 1024 * 1024

A_CHUNK = 128
A_GROUPS = 8
B_KSIZE = 3
C_WINDOWS = (2, 4, 8, 16)
D_HEADS = 16
D_LAT = 128
D_VDIM = 64
D_IDX_HEADS = 8
D_IDX_DIM = 64
D_TOPK_MAX = 256

INT_MIN = -(2 ** 31)
NEG_BIAS = -1e30
M_INIT = -1e29
LOG2E = 1.4426950408889634

TOKEN_TILE = 512
DSA_Q_TILE = 128
DSA_K_TILE = 512
DSA_SCORE_ROWS = 256
ATT_ROWS = 256
SOFTMAX_ROWS = 32
SUBLANES = 8

f32 = jnp.float32
bf16 = jnp.bfloat16
i32 = jnp.int32


def _dot(a, b):
    return jnp.dot(a, b, preferred_element_type=f32)


def _dot_nt(a, b):
    return lax.dot_general(a, b, (((1,), (1,)), ((), ())), preferred_element_type=f32)


def _rms(x, g):
    ms = jnp.mean(x * x, axis=-1, keepdims=True)
    return x * lax.rsqrt(ms + EPS) * g


def _params(*sem):
    return pltpu.CompilerParams(dimension_semantics=sem, vmem_limit_bytes=VMEM_LIMIT_BYTES)


def _full(shape):
    nd = len(shape)
    return pl.BlockSpec(shape, lambda *_: (0,) * nd)


def _mlp_kernel(x_ref, g_ref, w1_ref, w2_ref, fg_ref, o_ref, *, ff_chunk, apply_final):
    x = x_ref[...]
    h = _rms(x, g_ref[...]).astype(bf16)
    acc = x
    for c in range(w1_ref.shape[1] // ff_chunk):
        a = jnp.maximum(_dot(h, w1_ref[:, c * ff_chunk:(c + 1) * ff_chunk]), 0.0)
        acc = acc + _dot((a * a).astype(bf16), w2_ref[c * ff_chunk:(c + 1) * ff_chunk, :])
    if apply_final:
        acc = _rms(acc, fg_ref[...])
    o_ref[...] = acc


def _mlp(x, g, w1, w2, final_g, apply_final):
    n, d = x.shape
    dff = w1.shape[1]
    tm = min(TOKEN_TILE, n)
    row = pl.BlockSpec((tm, d), lambda i: (i, 0))
    return pl.pallas_call(
        functools.partial(_mlp_kernel, ff_chunk=min(1024, dff), apply_final=apply_final),
        grid=(n // tm,),
        in_specs=[row, _full((1, d)), _full((d, dff)), _full((dff, d)), _full((1, d))],
        out_specs=row,
        out_shape=jax.ShapeDtypeStruct((n, d), f32),
        compiler_params=_params("parallel"),
    )(x, g.reshape(1, d), w1.astype(bf16), w2.astype(bf16), final_g.reshape(1, d))


def _sgu_kernel(x_ref, g_ref, win_ref, vg_ref, ws_ref, bst_ref, wout_ref, o_ref, s_ref):
    tm, _ = x_ref.shape
    width = wout_ref.shape[0]
    gdim = width // A_GROUPS
    x = x_ref[...]
    h = _rms(x, g_ref[...]).astype(bf16)
    u = jax.nn.gelu(_dot(h, win_ref[:, :width]))
    v = jax.nn.gelu(_dot(h, win_ref[:, width:]))
    v = _rms(v, vg_ref[...]).astype(bf16)
    r = lax.broadcasted_iota(i32, (A_CHUNK, A_CHUNK), 0)
    c = lax.broadcasted_iota(i32, (A_CHUNK, A_CHUNK), 1)
    causal = c <= r
    bst = bst_ref[...]
    for gi in range(A_GROUPS):
        ws = jnp.where(causal, ws_ref[gi], 0.0).astype(bf16)
        bias = jnp.broadcast_to(bst[:, gi:gi + 1], (A_CHUNK, gdim))
        for n in range(tm // A_CHUNK):
            vb = v[n * A_CHUNK:(n + 1) * A_CHUNK, gi * gdim:(gi + 1) * gdim]
            s_ref[n * A_CHUNK:(n + 1) * A_CHUNK, gi * gdim:(gi + 1) * gdim] = _dot(ws, vb) + bias
    o_ref[...] = x + _dot((u * s_ref[...]).astype(bf16), wout_ref[...])


def _mixer_sgu(x, g, w_in, v_g, w_s, b_s, w_out):
    n, d = x.shape
    width = w_out.shape[0]
    tm = min(TOKEN_TILE, n)
    row = pl.BlockSpec((tm, d), lambda i: (i, 0))
    return pl.pallas_call(
        _sgu_kernel,
        grid=(n // tm,),
        in_specs=[row, _full((1, d)), _full((d, 2 * width)), _full((1, width)),
                  _full(w_s.shape), _full((A_CHUNK, A_GROUPS)), _full((width, d))],
        out_specs=row,
        out_shape=jax.ShapeDtypeStruct((n, d), f32),
        scratch_shapes=[pltpu.VMEM((tm, width), f32)],
        compiler_params=_params("parallel"),
    )(x, g.reshape(1, d), w_in.astype(bf16), v_g.reshape(1, width), w_s, b_s.T, w_out.astype(bf16))


CONV_HALO = 8


def _conv_kernel(x_ref, g_ref, win_ref, cw_ref, wout_ref, o_ref, z_ref):
    tm = x_ref.shape[1]
    width = wout_ref.shape[0]

    @pl.when(pl.program_id(1) == 0)
    def _():
        z_ref[0:CONV_HALO, :] = jnp.zeros((CONV_HALO, width), f32)

    x = x_ref[0]
    h = _rms(x, g_ref[...]).astype(bf16)
    bg = _dot(h, win_ref[:, :width])
    z = _dot(h, win_ref[:, width:2 * width]) * _dot(h, win_ref[:, 2 * width:])
    z_ref[CONV_HALO:CONV_HALO + tm, :] = z
    y = cw_ref[B_KSIZE - 1:B_KSIZE, :] * z
    for k in range(B_KSIZE - 1):
        shift = B_KSIZE - 1 - k
        y = y + cw_ref[k:k + 1, :] * z_ref[CONV_HALO - shift:CONV_HALO - shift + tm, :]
    z_ref[0:CONV_HALO, :] = z[tm - CONV_HALO:, :]
    o_ref[0] = x + _dot((bg * y).astype(bf16), wout_ref[...])


def _mixer_conv(x, g, w_in, conv_w, w_out):
    b, l, d = x.shape
    width = w_out.shape[0]
    tm = min(TOKEN_TILE, l)
    row = pl.BlockSpec((1, tm, d), lambda i, j: (i, j, 0))
    return pl.pallas_call(
        _conv_kernel,
        grid=(b, l // tm),
        in_specs=[row, _full((1, d)), _full((d, 3 * width)), _full((B_KSIZE, width)), _full((width, d))],
        out_specs=row,
        out_shape=jax.ShapeDtypeStruct((b, l, d), f32),
        scratch_shapes=[pltpu.VMEM((CONV_HALO + tm, width), f32)],
        compiler_params=_params("arbitrary", "arbitrary"),
    )(x, g.reshape(1, d), w_in.astype(bf16), conv_w.reshape(B_KSIZE, width), w_out.astype(bf16))


POOL_HALO = 16
POOL_PAD = 8


def _pool_kernel(x_ref, g_ref, win_ref, wgrp_ref, sc_ref, o_ref, carry_ref, pa_ref, pb_ref, y_ref):
    tm = x_ref.shape[1]
    d = x_ref.shape[2]
    gdim = d // len(C_WINDOWS)
    n_ext = tm + POOL_HALO
    j = pl.program_id(1)

    @pl.when(j == 0)
    def _():
        carry_ref[...] = jnp.zeros(carry_ref.shape, f32)

    x = x_ref[0]
    h = _rms(x, g_ref[...]).astype(bf16)
    z = _dot(h, win_ref[...])
    pa_ref[0:POOL_PAD, :] = jnp.zeros((POOL_PAD, gdim), f32)
    pb_ref[0:POOL_PAD, :] = jnp.zeros((POOL_PAD, gdim), f32)
    pos = j * tm + lax.broadcasted_iota(i32, (tm, 1), 0) + 1
    for gi, w in enumerate(C_WINDOWS):
        zg = z[:, gi * gdim:(gi + 1) * gdim]
        src, dst = pa_ref, pb_ref
        src[POOL_PAD:POOL_PAD + POOL_HALO, :] = carry_ref[:, gi * gdim:(gi + 1) * gdim]
        src[POOL_PAD + POOL_HALO:, :] = zg
        shift = 1
        while True:
            acc = src[POOL_PAD:, :] + src[POOL_PAD - shift:POOL_PAD - shift + n_ext, :]
            shift *= 2
            if shift == w:
                break
            dst[POOL_PAD:, :] = acc
            src, dst = dst, src
        cnt = jnp.minimum(pos, w).astype(f32)
        pooled = acc[POOL_HALO:, :] / cnt - zg
        y_ref[:, gi * gdim:(gi + 1) * gdim] = _dot(pooled.astype(bf16), wgrp_ref[gi])
    carry_ref[...] = z[tm - POOL_HALO:, :]
    o_ref[0] = x + y_ref[...] * sc_ref[...]


def _mixer_pool(x, g, w_in, w_grp, scale):
    b, l, d = x.shape
    gdim = d // len(C_WINDOWS)
    tm = min(TOKEN_TILE, l)
    row = pl.BlockSpec((1, tm, d), lambda i, j: (i, j, 0))
    ext = POOL_PAD + POOL_HALO + tm
    return pl.pallas_call(
        _pool_kernel,
        grid=(b, l // tm),
        in_specs=[row, _full((1, d)), _full((d, d)), _full(w_grp.shape), _full((1, d))],
        out_specs=row,
        out_shape=jax.ShapeDtypeStruct((b, l, d), f32),
        scratch_shapes=[pltpu.VMEM((POOL_HALO, d), f32), pltpu.VMEM((ext, gdim), f32),
                        pltpu.VMEM((ext, gdim), f32), pltpu.VMEM((tm, d), f32)],
        compiler_params=_params("arbitrary", "arbitrary"),
    )(x, g.reshape(1, d), w_in.astype(bf16), w_grp.astype(bf16), scale.reshape(1, d))


def _dsa_proj_kernel(x_ref, g_ref, wq_ref, wc_ref, wqi_ref, wkw_ref, kvg_ref,
                     q_ref, c_ref, qi_ref, ki_ref, wit_ref):
    tm = x_ref.shape[1]
    h = _rms(x_ref[0], g_ref[...]).astype(bf16)
    q = _dot(h, wq_ref[...])
    for hd in range(D_HEADS):
        q_ref[0, hd] = q[:, hd * D_LAT:(hd + 1) * D_LAT].astype(bf16)
    c = _rms(_dot(h, wc_ref[...]), kvg_ref[...]).astype(bf16)
    c_ref[0, :, :D_LAT] = c
    c_ref[0, :, D_LAT:] = jnp.ones((tm, D_LAT), bf16)
    qi = _dot(h, wqi_ref[...])
    for hd in range(D_IDX_HEADS):
        qi_ref[0, hd] = qi[:, hd * D_IDX_DIM:(hd + 1) * D_IDX_DIM].astype(bf16)
    kw = _dot(h, wkw_ref[...])
    ki_ref[0] = kw[:, :D_IDX_DIM].astype(bf16)
    wit_ref[0] = kw.T[D_IDX_DIM:D_IDX_DIM + D_IDX_HEADS, :] * (D_IDX_HEADS ** -0.5 * D_IDX_DIM ** -0.5)


def _dsa_proj(x, g, w_in, kv_g):
    b, l, d = x.shape
    nq = D_HEADS * D_LAT
    nqi = D_IDX_HEADS * D_IDX_DIM
    o_c, o_qi, o_ki = nq, nq + D_LAT, nq + D_LAT + nqi
    w_in = w_in.astype(bf16)
    w_kw = jnp.pad(w_in[:, o_ki:], ((0, 0), (0, LANES - (D_IDX_DIM + D_IDX_HEADS))))
    tm = min(TOKEN_TILE, l)
    row = pl.BlockSpec((1, tm, d), lambda i, j: (i, j, 0))
    return pl.pallas_call(
        _dsa_proj_kernel,
        grid=(b, l // tm),
        in_specs=[row, _full((1, d)), _full((d, nq)), _full((d, D_LAT)), _full((d, nqi)),
                  _full((d, LANES)), _full((1, D_LAT))],
        out_specs=[pl.BlockSpec((1, D_HEADS, tm, D_LAT), lambda i, j: (i, 0, j, 0)),
                   pl.BlockSpec((1, tm, 2 * D_LAT), lambda i, j: (i, j, 0)),
                   pl.BlockSpec((1, D_IDX_HEADS, tm, D_IDX_DIM), lambda i, j: (i, 0, j, 0)),
                   pl.BlockSpec((1, tm, D_IDX_DIM), lambda i, j: (i, j, 0)),
                   pl.BlockSpec((1, D_IDX_HEADS, tm), lambda i, j: (i, 0, j))],
        out_shape=[jax.ShapeDtypeStruct((b, D_HEADS, l, D_LAT), bf16),
                   jax.ShapeDtypeStruct((b, l, 2 * D_LAT), bf16),
                   jax.ShapeDtypeStruct((b, D_IDX_HEADS, l, D_IDX_DIM), bf16),
                   jax.ShapeDtypeStruct((b, l, D_IDX_DIM), bf16),
                   jax.ShapeDtypeStruct((b, D_IDX_HEADS, l), f32)],
        compiler_params=_params("parallel", "parallel"),
    )(x, g.reshape(1, d), w_in[:, :o_c], w_in[:, o_c:o_qi], w_in[:, o_qi:o_ki], w_kw,
      kv_g.reshape(1, D_LAT))


def _lane_tile(a, n):
    return a if n == LANES else jnp.concatenate([a] * (n // LANES), axis=1)


def _dsa_attn_kernel(q_ref, qi_ref, wit_ref, c_ref, ki_ref, o_ref,
                     key_ref, qx_ref, lg_ref, p_ref, m_ref, alpha_ref, acc_ref, *, topk):
    tq = q_ref.shape[2]
    tk = DSA_K_TILE
    ts = DSA_SCORE_ROWS
    nrow = D_HEADS * tq
    t0 = pl.program_id(1) * tq
    n_kt = (t0 + tq + tk - 1) // tk
    n_st = n_kt * (tk // ts)
    kf = float(topk)

    qi = qi_ref[0].reshape(D_IDX_HEADS * tq, D_IDX_DIM)
    wit = wit_ref[0]
    key_pos = lax.broadcasted_iota(i32, (ts, tq), 0)
    qry_pos = t0 + lax.broadcasted_iota(i32, (ts, tq), 1)

    def score_body(j, carry):
        k0 = pl.multiple_of(j * ts, ts)
        lg = _dot_nt(ki_ref[0, pl.ds(k0, ts), :], qi)
        s = None
        for hd in range(D_IDX_HEADS):
            term = jnp.maximum(lg[:, hd * tq:(hd + 1) * tq], 0.0) * wit[hd:hd + 1, :]
            s = term if s is None else s + term
        s = jnp.where(s == 0.0, 0.0, s)
        bits = pltpu.bitcast(s, i32)
        key = bits ^ ((bits >> 31) & 0x7FFFFFFF)
        key_ref[pl.ds(k0, ts), :] = jnp.where(k0 + key_pos <= qry_pos, key, INT_MIN)
        return carry

    lax.fori_loop(0, n_st, score_body, 0)

    def count_ge(cand):
        def body(j, acc):
            k0 = pl.multiple_of(j * ts, ts)
            blk = key_ref[pl.ds(k0, ts), :].reshape(ts // (4 * SUBLANES), 4, SUBLANES, tq)
            return acc + jnp.sum(jnp.where(blk >= cand[None, None], 1.0, 0.0), axis=0)

        acc = lax.fori_loop(0, n_st, body, jnp.zeros((4, SUBLANES, tq), f32))
        return jnp.sum(jnp.sum(acc, axis=0), axis=0, keepdims=True)

    cnt0 = count_ge(jnp.zeros((SUBLANES, tq), i32))
    thr = jnp.where(cnt0 >= kf, jnp.zeros((SUBLANES, tq), i32), INT_MIN)
    cnt_thr = jnp.where(cnt0 >= kf, cnt0, 0.0)

    def bit_body(i, carry):
        thr, cnt_thr = carry
        cand = thr | lax.shift_left(jnp.int32(1), 30 - i)
        cnt = count_ge(cand)
        ok = cnt >= kf
        return jnp.where(ok, cand, thr), jnp.where(ok, cnt, cnt_thr)

    thr, cnt_thr = lax.fori_loop(0, 31, bit_body, (thr, cnt_thr))
    thr = jnp.maximum(thr, INT_MIN + 1)
    thr_row = thr[0:1, :]

    @pl.when(jnp.max(cnt_thr) > kf)
    def _():
        keep_n = kf - count_ge(thr + 1)
        tri = (lax.broadcasted_iota(i32, (LANES, LANES), 1)
               <= lax.broadcasted_iota(i32, (LANES, LANES), 0)).astype(bf16)

        def tie_body(j, run):
            k0 = pl.multiple_of(j * LANES, LANES)
            blk = key_ref[pl.ds(k0, LANES), :]
            tie = jnp.where(blk == thr_row, 1.0, 0.0)
            rank = run + _dot(tri, tie.astype(bf16))
            drop = (tie * jnp.where(rank > keep_n, 1.0, 0.0)) > 0.5
            key_ref[pl.ds(k0, LANES), :] = jnp.where(drop, INT_MIN, blk)
            return run + jnp.sum(tie, axis=0, keepdims=True)

        lax.fori_loop(0, n_kt * (tk // LANES), tie_body, jnp.zeros((1, tq), f32))

    m_ref[...] = jnp.full(m_ref.shape, M_INIT, f32)
    acc_ref[...] = jnp.zeros(acc_ref.shape, f32)
    qx_ref[:, :D_LAT] = q_ref[0].reshape(nrow, D_LAT)
    onehot = ((lax.broadcasted_iota(i32, (nrow, tq), 0) & (tq - 1))
              == lax.broadcasted_iota(i32, (nrow, tq), 1))
    qx_ref[:, D_LAT:] = jnp.where(onehot, 1.0, 0.0).astype(bf16)
    scale = D_LAT ** -0.5 * LOG2E

    def att_body(j, carry):
        k0 = pl.multiple_of(j * tk, tk)
        ce = c_ref[0, pl.ds(k0, tk), :]
        mask_t = jnp.where(key_ref[pl.ds(k0, tk), :] >= thr_row, 0.0, NEG_BIAS).astype(bf16)
        rhs = jnp.concatenate([ce[:, :D_LAT], mask_t], axis=1)
        for r0 in range(0, nrow, ATT_ROWS):
            ra = slice(r0, r0 + ATT_ROWS)
            lg_ref[ra, :] = _dot_nt(qx_ref[ra, :], rhs)
            for r1 in range(r0, r0 + ATT_ROWS, SOFTMAX_ROWS):
                rs = slice(r1, r1 + SOFTMAX_ROWS)
                lh = lg_ref[rs, :]
                m_prev = m_ref[rs, :]
                m_new = jnp.maximum(m_prev, jnp.max(lh, axis=1, keepdims=True))
                alpha_ref[rs, :] = jnp.exp2((m_prev - m_new) * scale)
                p_ref[rs, :] = jnp.exp2((lh - _lane_tile(m_new, tk)) * scale).astype(bf16)
                m_ref[rs, :] = m_new
            alpha = alpha_ref[ra, :]
            acc_ref[ra, :] = acc_ref[ra, :] * jnp.concatenate([alpha, alpha], axis=1) + _dot(p_ref[ra, :], ce)
        return carry

    lax.fori_loop(0, n_kt, att_body, 0)

    for hd in range(D_HEADS):
        rs = slice(hd * tq, (hd + 1) * tq)
        o_ref[0, :, hd * D_LAT:(hd + 1) * D_LAT] = (acc_ref[rs, :D_LAT] / acc_ref[rs, D_LAT:]).astype(bf16)


def _dsa_attn(q, qi, wit, c_ext, ki, topk):
    b, _, l, _ = q.shape
    tq = DSA_Q_TILE
    assert tq == LANES and l % DSA_K_TILE == 0 and DSA_K_TILE % DSA_SCORE_ROWS == 0
    return pl.pallas_call(
        functools.partial(_dsa_attn_kernel, topk=topk),
        grid=(b, l // tq),
        in_specs=[pl.BlockSpec((1, D_HEADS, tq, D_LAT), lambda i, j: (i, 0, j, 0)),
                  pl.BlockSpec((1, D_IDX_HEADS, tq, D_IDX_DIM), lambda i, j: (i, 0, j, 0)),
                  pl.BlockSpec((1, D_IDX_HEADS, tq), lambda i, j: (i, 0, j)),
                  pl.BlockSpec((1, l, 2 * D_LAT), lambda i, j: (i, 0, 0)),
                  pl.BlockSpec((1, l, D_IDX_DIM), lambda i, j: (i, 0, 0))],
        out_specs=pl.BlockSpec((1, tq, D_HEADS * D_LAT), lambda i, j: (i, j, 0)),
        out_shape=jax.ShapeDtypeStruct((b, l, D_HEADS * D_LAT), bf16),
        scratch_shapes=[pltpu.VMEM((l, tq), i32),
                        pltpu.VMEM((D_HEADS * tq, D_LAT + tq), bf16),
                        pltpu.VMEM((D_HEADS * tq, DSA_K_TILE), f32),
                        pltpu.VMEM((D_HEADS * tq, DSA_K_TILE), bf16),
                        pltpu.VMEM((D_HEADS * tq, LANES), f32),
                        pltpu.VMEM((D_HEADS * tq, LANES), f32),
                        pltpu.VMEM((D_HEADS * tq, 2 * D_LAT), f32)],
        compiler_params=_params("parallel", "arbitrary"),
    )(q, qi, wit, c_ext, ki)


def _dsa_out_kernel(x_ref, o_ref, wuv_ref, wout_ref, y_ref, ov_ref):
    for hd in range(D_HEADS):
        ov_ref[:, hd * D_VDIM:(hd + 1) * D_VDIM] = _dot(o_ref[:, hd * D_LAT:(hd + 1) * D_LAT], wuv_ref[hd])
    y_ref[...] = x_ref[...] + _dot(ov_ref[...].astype(bf16), wout_ref[...])


def _dsa_out(x, o, w_uv, w_out):
    n, d = x.shape
    tm = min(TOKEN_TILE, n)
    return pl.pallas_call(
        _dsa_out_kernel,
        grid=(n // tm,),
        in_specs=[pl.BlockSpec((tm, d), lambda i: (i, 0)),
                  pl.BlockSpec((tm, D_HEADS * D_LAT), lambda i: (i, 0)),
                  _full(w_uv.shape), _full(w_out.shape)],
        out_specs=pl.BlockSpec((tm, d), lambda i: (i, 0)),
        out_shape=jax.ShapeDtypeStruct((n, d), f32),
        scratch_shapes=[pltpu.VMEM((tm, D_HEADS * D_VDIM), f32)],
        compiler_params=_params("parallel"),
    )(x, o, w_uv.astype(bf16), w_out.astype(bf16))


def _mixer_dsa(x, g, w_in, kv_g, w_uv, w_out):
    b, l, d = x.shape
    topk = min(D_TOPK_MAX, l // 4)
    q, c_ext, qi, ki, wit = _dsa_proj(x, g, w_in, kv_g)
    o = _dsa_attn(q, qi, wit, c_ext, ki, topk)
    return _dsa_out(x.reshape(b * l, d), o.reshape(b * l, D_HEADS * D_LAT), w_uv, w_out).reshape(b, l, d)


@jax.jit
def _trunk(x, norm_mix_g, norm_mlp_g, final_g, a_w_in, a_v_g, a_w_s, a_b_s, a_w_out, b_w_in, b_conv_w,
           b_w_out, c_w_in, c_w_grp, c_scale, d_w_in, d_kv_g, d_w_uv, d_w_out, mlp_w1, mlp_w2):
    b, l, d = x.shape
    depth = norm_mix_g.shape[0]
    n_mixers = 4
    for i in range(depth):
        m, j = i % n_mixers, i // n_mixers
        g = norm_mix_g[i]
        if m == 0:
            x = _mixer_sgu(x.reshape(b * l, d), g, a_w_in[j], a_v_g[j], a_w_s[j], a_b_s[j],
                           a_w_out[j]).reshape(b, l, d)
        elif m == 1:
            x = _mixer_conv(x, g, b_w_in[j], b_conv_w[j], b_w_out[j])
        elif m == 2:
            x = _mixer_pool(x, g, c_w_in[j], c_w_grp[j], c_scale[j])
        else:
            x = _mixer_dsa(x, g, d_w_in[j], d_kv_g[j], d_w_uv[j], d_w_out[j])
        x = _mlp(x.reshape(b * l, d), norm_mlp_g[i], mlp_w1[i], mlp_w2[i], final_g,
                 apply_final=(i == depth - 1)).reshape(b, l, d)
    return x


def kernel(x, norm_mix_g, norm_mlp_g, final_g, a_w_in, a_v_g, a_w_s, a_b_s, a_w_out, b_w_in, b_conv_w,
           b_w_out, c_w_in, c_w_grp, c_scale, d_w_in, d_kv_g, d_w_uv, d_w_out, mlp_w1, mlp_w2):
    return _trunk(x, norm_mix_g, norm_mlp_g, final_g, a_w_in, a_v_g, a_w_s, a_b_s, a_w_out, b_w_in,
                  b_conv_w, b_w_out, c_w_in, c_w_grp, c_scale, d_w_in, d_kv_g, d_w_uv, d_w_out,
                  mlp_w1, mlp_w2)
```

```python
import functools

import jax
import jax.numpy as jnp
from jax import lax
from jax.experimental import pallas as pl
from jax.experimental.pallas import tpu as pltpu

EPS = 1e-6
LANES = 128
VMEM_LIMIT_BYTES = 56 * 1024 * 1024

A_CHUNK = 128
A_GROUPS = 8
B_KSIZE = 3
C_WINDOWS = (2, 4, 8, 16)
D_HEADS = 16
D_LAT = 128
D_VDIM = 64
D_IDX_HEADS = 8
D_IDX_DIM = 64
D_TOPK_MAX = 256

INT_MIN = -(2 ** 31)
NEG_BIAS = -1e30
M_INIT = -1e29
LOG2E = 1.4426950408889634

TOKEN_TILE = 512
DSA_Q_TILE = 128
DSA_K_TILE = 512
DSA_SCORE_ROWS = 256
ATT_ROWS = 256
SOFTMAX_ROWS = 32
SUBLANES = 8
WORD_BITS = 32

f32 = jnp.float32
bf16 = jnp.bfloat16
i32 = jnp.int32


def _dot(a, b):
    return jnp.dot(a, b, preferred_element_type=f32)


def _dot_nt(a, b):
    return lax.dot_general(a, b, (((1,), (1,)), ((), ())), preferred_element_type=f32)


def _rms(x, g):
    ms = jnp.mean(x * x, axis=-1, keepdims=True)
    return x * lax.rsqrt(ms + EPS) * g


def _params(*sem):
    return pltpu.CompilerParams(dimension_semantics=sem, vmem_limit_bytes=VMEM_LIMIT_BYTES)


def _full(shape):
    nd = len(shape)
    return pl.BlockSpec(shape, lambda *_: (0,) * nd)


def _mlp_kernel(x_ref, g_ref, w1_ref, w2_ref, fg_ref, o_ref, *, ff_chunk, apply_final):
    x = x_ref[...]
    h = _rms(x, g_ref[...]).astype(bf16)
    acc = x
    for c in range(w1_ref.shape[1] // ff_chunk):
        a = jnp.maximum(_dot(h, w1_ref[:, c * ff_chunk:(c + 1) * ff_chunk]), 0.0)
        acc = acc + _dot((a * a).astype(bf16), w2_ref[c * ff_chunk:(c + 1) * ff_chunk, :])
    if apply_final:
        acc = _rms(acc, fg_ref[...])
    o_ref[...] = acc


def _mlp(x, g, w1, w2, final_g, apply_final):
    n, d = x.shape
    dff = w1.shape[1]
    tm = min(TOKEN_TILE, n)
    row = pl.BlockSpec((tm, d), lambda i: (i, 0))
    return pl.pallas_call(
        functools.partial(_mlp_kernel, ff_chunk=min(1024, dff), apply_final=apply_final),
        grid=(n // tm,),
        in_specs=[row, _full((1, d)), _full((d, dff)), _full((dff, d)), _full((1, d))],
        out_specs=row,
        out_shape=jax.ShapeDtypeStruct((n, d), f32),
        compiler_params=_params("parallel"),
    )(x, g.reshape(1, d), w1.astype(bf16), w2.astype(bf16), final_g.reshape(1, d))


def _sgu_kernel(x_ref, g_ref, win_ref, vg_ref, ws_ref, bst_ref, wout_ref, o_ref, s_ref):
    tm, _ = x_ref.shape
    width = wout_ref.shape[0]
    gdim = width // A_GROUPS
    x = x_ref[...]
    h = _rms(x, g_ref[...]).astype(bf16)
    u = jax.nn.gelu(_dot(h, win_ref[:, :width]))
    v = jax.nn.gelu(_dot(h, win_ref[:, width:]))
    v = _rms(v, vg_ref[...]).astype(bf16)
    r = lax.broadcasted_iota(i32, (A_CHUNK, A_CHUNK), 0)
    c = lax.broadcasted_iota(i32, (A_CHUNK, A_CHUNK), 1)
    causal = c <= r
    bst = bst_ref[...]
    for gi in range(A_GROUPS):
        ws = jnp.where(causal, ws_ref[gi], 0.0).astype(bf16)
        bias = jnp.broadcast_to(bst[:, gi:gi + 1], (A_CHUNK, gdim))
        for n in range(tm // A_CHUNK):
            vb = v[n * A_CHUNK:(n + 1) * A_CHUNK, gi * gdim:(gi + 1) * gdim]
            s_ref[n * A_CHUNK:(n + 1) * A_CHUNK, gi * gdim:(gi + 1) * gdim] = _dot(ws, vb) + bias
    o_ref[...] = x + _dot((u * s_ref[...]).astype(bf16), wout_ref[...])


def _mixer_sgu(x, g, w_in, v_g, w_s, b_s, w_out):
    n, d = x.shape
    width = w_out.shape[0]
    tm = min(TOKEN_TILE, n)
    row = pl.BlockSpec((tm, d), lambda i: (i, 0))
    return pl.pallas_call(
        _sgu_kernel,
        grid=(n // tm,),
        in_specs=[row, _full((1, d)), _full((d, 2 * width)), _full((1, width)),
                  _full(w_s.shape), _full((A_CHUNK, A_GROUPS)), _full((width, d))],
        out_specs=row,
        out_shape=jax.ShapeDtypeStruct((n, d), f32),
        scratch_shapes=[pltpu.VMEM((tm, width), f32)],
        compiler_params=_params("parallel"),
    )(x, g.reshape(1, d), w_in.astype(bf16), v_g.reshape(1, width), w_s, b_s.T, w_out.astype(bf16))


CONV_HALO = 8


def _conv_kernel(x_ref, g_ref, win_ref, cw_ref, wout_ref, o_ref, z_ref):
    tm = x_ref.shape[1]
    width = wout_ref.shape[0]

    @pl.when(pl.program_id(1) == 0)
    def _():
        z_ref[0:CONV_HALO, :] = jnp.zeros((CONV_HALO, width), f32)

    x = x_ref[0]
    h = _rms(x, g_ref[...]).astype(bf16)
    bg = _dot(h, win_ref[:, :width])
    z = _dot(h, win_ref[:, width:2 * width]) * _dot(h, win_ref[:, 2 * width:])
    z_ref[CONV_HALO:CONV_HALO + tm, :] = z
    y = cw_ref[B_KSIZE - 1:B_KSIZE, :] * z
    for k in range(B_KSIZE - 1):
        shift = B_KSIZE - 1 - k
        y = y + cw_ref[k:k + 1, :] * z_ref[CONV_HALO - shift:CONV_HALO - shift + tm, :]
    z_ref[0:CONV_HALO, :] = z[tm - CONV_HALO:, :]
    o_ref[0] = x + _dot((bg * y).astype(bf16), wout_ref[...])


def _mixer_conv(x, g, w_in, conv_w, w_out):
    b, l, d = x.shape
    width = w_out.shape[0]
    tm = min(TOKEN_TILE, l)
    row = pl.BlockSpec((1, tm, d), lambda i, j: (i, j, 0))
    return pl.pallas_call(
        _conv_kernel,
        grid=(b, l // tm),
        in_specs=[row, _full((1, d)), _full((d, 3 * width)), _full((B_KSIZE, width)), _full((width, d))],
        out_specs=row,
        out_shape=jax.ShapeDtypeStruct((b, l, d), f32),
        scratch_shapes=[pltpu.VMEM((CONV_HALO + tm, width), f32)],
        compiler_params=_params("arbitrary", "arbitrary"),
    )(x, g.reshape(1, d), w_in.astype(bf16), conv_w.reshape(B_KSIZE, width), w_out.astype(bf16))


POOL_HALO = 16
POOL_PAD = 8


def _pool_kernel(x_ref, g_ref, win_ref, wgrp_ref, sc_ref, o_ref, carry_ref, pa_ref, pb_ref, y_ref):
    tm = x_ref.shape[1]
    d = x_ref.shape[2]
    gdim = d // len(C_WINDOWS)
    n_ext = tm + POOL_HALO
    j = pl.program_id(1)

    @pl.when(j == 0)
    def _():
        carry_ref[...] = jnp.zeros(carry_ref.shape, f32)

    x = x_ref[0]
    h = _rms(x, g_ref[...]).astype(bf16)
    z = _dot(h, win_ref[...])
    pa_ref[0:POOL_PAD, :] = jnp.zeros((POOL_PAD, gdim), f32)
    pb_ref[0:POOL_PAD, :] = jnp.zeros((POOL_PAD, gdim), f32)
    pos = j * tm + lax.broadcasted_iota(i32, (tm, 1), 0) + 1
    for gi, w in enumerate(C_WINDOWS):
        zg = z[:, gi * gdim:(gi + 1) * gdim]
        src, dst = pa_ref, pb_ref
        src[POOL_PAD:POOL_PAD + POOL_HALO, :] = carry_ref[:, gi * gdim:(gi + 1) * gdim]
        src[POOL_PAD + POOL_HALO:, :] = zg
        shift = 1
        while True:
            acc = src[POOL_PAD:, :] + src[POOL_PAD - shift:POOL_PAD - shift + n_ext, :]
            shift *= 2
            if shift == w:
                break
            dst[POOL_PAD:, :] = acc
            src, dst = dst, src
        cnt = jnp.minimum(pos, w).astype(f32)
        pooled = acc[POOL_HALO:, :] / cnt - zg
        y_ref[:, gi * gdim:(gi + 1) * gdim] = _dot(pooled.astype(bf16), wgrp_ref[gi])
    carry_ref[...] = z[tm - POOL_HALO:, :]
    o_ref[0] = x + y_ref[...] * sc_ref[...]


def _mixer_pool(x, g, w_in, w_grp, scale):
    b, l, d = x.shape
    gdim = d // len(C_WINDOWS)
    tm = min(TOKEN_TILE, l)
    row = pl.BlockSpec((1, tm, d), lambda i, j: (i, j, 0))
    ext = POOL_PAD + POOL_HALO + tm
    return pl.pallas_call(
        _pool_kernel,
        grid=(b, l // tm),
        in_specs=[row, _full((1, d)), _full((d, d)), _full(w_grp.shape), _full((1, d))],
        out_specs=row,
        out_shape=jax.ShapeDtypeStruct((b, l, d), f32),
        scratch_shapes=[pltpu.VMEM((POOL_HALO, d), f32), pltpu.VMEM((ext, gdim), f32),
                        pltpu.VMEM((ext, gdim), f32), pltpu.VMEM((tm, d), f32)],
        compiler_params=_params("arbitrary", "arbitrary"),
    )(x, g.reshape(1, d), w_in.astype(bf16), w_grp.astype(bf16), scale.reshape(1, d))


def _dsa_proj_kernel(x_ref, g_ref, wq_ref, wc_ref, wqi_ref, wkw_ref, kvg_ref,
                     q_ref, c_ref, qi_ref, ki_ref, wit_ref):
    tm = x_ref.shape[1]
    h = _rms(x_ref[0], g_ref[...]).astype(bf16)
    q = _dot(h, wq_ref[...])
    for hd in range(D_HEADS):
        q_ref[0, hd] = q[:, hd * D_LAT:(hd + 1) * D_LAT].astype(bf16)
    c = _rms(_dot(h, wc_ref[...]), kvg_ref[...]).astype(bf16)
    c_ref[0, :, :D_LAT] = c
    c_ref[0, :, D_LAT:] = jnp.ones((tm, D_LAT), bf16)
    qi = _dot(h, wqi_ref[...])
    for hd in range(D_IDX_HEADS):
        qi_ref[0, hd] = qi[:, hd * D_IDX_DIM:(hd + 1) * D_IDX_DIM].astype(bf16)
    kw = _dot(h, wkw_ref[...])
    ki_ref[0] = kw[:, :D_IDX_DIM].astype(bf16)
    wit_ref[0] = kw.T[D_IDX_DIM:D_IDX_DIM + D_IDX_HEADS, :] * (D_IDX_HEADS ** -0.5 * D_IDX_DIM ** -0.5)


def _dsa_proj(x, g, w_in, kv_g):
    b, l, d = x.shape
    nq = D_HEADS * D_LAT
    nqi = D_IDX_HEADS * D_IDX_DIM
    o_c, o_qi, o_ki = nq, nq + D_LAT, nq + D_LAT + nqi
    w_in = w_in.astype(bf16)
    w_kw = jnp.pad(w_in[:, o_ki:], ((0, 0), (0, LANES - (D_IDX_DIM + D_IDX_HEADS))))
    tm = min(TOKEN_TILE, l)
    row = pl.BlockSpec((1, tm, d), lambda i, j: (i, j, 0))
    return pl.pallas_call(
        _dsa_proj_kernel,
        grid=(b, l // tm),
        in_specs=[row, _full((1, d)), _full((d, nq)), _full((d, D_LAT)), _full((d, nqi)),
                  _full((d, LANES)), _full((1, D_LAT))],
        out_specs=[pl.BlockSpec((1, D_HEADS, tm, D_LAT), lambda i, j: (i, 0, j, 0)),
                   pl.BlockSpec((1, tm, 2 * D_LAT), lambda i, j: (i, j, 0)),
                   pl.BlockSpec((1, D_IDX_HEADS, tm, D_IDX_DIM), lambda i, j: (i, 0, j, 0)),
                   pl.BlockSpec((1, tm, D_IDX_DIM), lambda i, j: (i, j, 0)),
                   pl.BlockSpec((1, D_IDX_HEADS, tm), lambda i, j: (i, 0, j))],
        out_shape=[jax.ShapeDtypeStruct((b, D_HEADS, l, D_LAT), bf16),
                   jax.ShapeDtypeStruct((b, l, 2 * D_LAT), bf16),
                   jax.ShapeDtypeStruct((b, D_IDX_HEADS, l, D_IDX_DIM), bf16),
                   jax.ShapeDtypeStruct((b, l, D_IDX_DIM), bf16),
                   jax.ShapeDtypeStruct((b, D_IDX_HEADS, l), f32)],
        compiler_params=_params("parallel", "parallel"),
    )(x, g.reshape(1, d), w_in[:, :o_c], w_in[:, o_c:o_qi], w_in[:, o_qi:o_ki], w_kw,
      kv_g.reshape(1, D_LAT))


def _bit_transpose(words):
    a = list(reversed(words))
    j, mask = WORD_BITS // 2, 0x0000FFFF
    while j:
        k = 0
        while k < WORD_BITS:
            t = (a[k] ^ lax.shift_right_logical(a[k + j], jnp.int32(j))) & mask
            a[k] = a[k] ^ t
            a[k + j] = a[k + j] ^ (t << j)
            k = (k + j + 1) & ~j
        j >>= 1
        mask ^= mask << j
    return a


def _lane_tile(a, n):
    return a if n == LANES else jnp.concatenate([a] * (n // LANES), axis=1)


def _dsa_attn_kernel(q_ref, qi_ref, wit_ref, c_ref, ki_ref, o_ref,
                     key_ref, plane_ref, qx_ref, lg_ref, p_ref, m_ref, alpha_ref, acc_ref, *, topk):
    tq = q_ref.shape[2]
    tk = DSA_K_TILE
    ts = DSA_SCORE_ROWS
    nrow = D_HEADS * tq
    n_word = plane_ref.shape[1]
    t0 = pl.program_id(1) * tq
    n_kt = (t0 + tq + tk - 1) // tk
    kf = float(topk)

    @pl.when((pl.program_id(0) == 0) & (pl.program_id(1) == 0))
    def _():
        plane_ref[...] = jnp.zeros(plane_ref.shape, i32)

    qi = qi_ref[0].reshape(D_IDX_HEADS * tq, D_IDX_DIM)
    wit = wit_ref[0]
    key_pos = lax.broadcasted_iota(i32, (ts, tq), 0)
    qry_pos = t0 + lax.broadcasted_iota(i32, (ts, tq), 1)

    def score_body(j, carry):
        for sub in range(tk // ts):
            k0 = pl.multiple_of(j * tk + sub * ts, ts)
            lg = _dot_nt(ki_ref[0, pl.ds(k0, ts), :], qi)
            s = None
            for hd in range(D_IDX_HEADS):
                term = jnp.maximum(lg[:, hd * tq:(hd + 1) * tq], 0.0) * wit[hd:hd + 1, :]
                s = term if s is None else s + term
            s = jnp.where(s == 0.0, 0.0, s)
            bits = pltpu.bitcast(s, i32)
            key = bits ^ ((bits >> 31) & 0x7FFFFFFF)
            key = jnp.where(k0 + key_pos <= qry_pos, key, INT_MIN)
            key_ref[pl.ds(k0, ts), :] = key
            unsigned = (key ^ INT_MIN).reshape(WORD_BITS, SUBLANES, tq)
            planes = _bit_transpose([unsigned[m] for m in range(WORD_BITS)])
            w0 = pl.multiple_of((j * (tk // ts) + sub) * SUBLANES, SUBLANES)
            for p in range(WORD_BITS):
                plane_ref[p, pl.ds(w0, SUBLANES), :] = planes[p]
        return carry

    lax.fori_loop(0, n_kt, score_body, 0)

    word = lax.broadcasted_iota(i32, (n_word, tq), 0)
    first_row = (word >> 3) * ts + (word & (SUBLANES - 1))
    n_valid = ((t0 + lax.broadcasted_iota(i32, (n_word, tq), 1) - first_row) >> 3) + 1
    cand0 = jnp.where(n_valid >= WORD_BITS, -1,
                      lax.shift_left(jnp.int32(1), jnp.clip(n_valid, 0, WORD_BITS - 1)) - 1)

    def row_sum(v):
        parts = [v[r:r + SUBLANES] for r in range(0, n_word, SUBLANES)]
        while len(parts) > 1:
            parts = [a + b for a, b in zip(parts[::2], parts[1::2])] + parts[len(parts) & ~1:]
        return jnp.sum(parts[0].astype(f32), axis=0, keepdims=True)

    def plane_body(p, carry):
        cand, need, thr = carry
        ones = cand & plane_ref[p]
        n_ones = row_sum(lax.population_count(ones))
        take = n_ones >= need
        thr = jnp.where(take, thr | lax.shift_left(jnp.int32(1), WORD_BITS - 1 - p), thr)
        return jnp.where(take, ones, cand ^ ones), jnp.where(take, need, need - n_ones), thr

    cand, keep_n, thr = lax.fori_loop(
        0, WORD_BITS, plane_body, (cand0, jnp.full((1, tq), kf, f32), jnp.zeros((1, tq), i32)))
    n_tied = row_sum(lax.population_count(cand))
    few = t0 + lax.broadcasted_iota(i32, (1, tq), 1) + 1 < topk
    thr_row = jnp.where(few, INT_MIN + 1, thr ^ INT_MIN)

    @pl.when(jnp.max(jnp.where(few, 0.0, n_tied - keep_n)) > 0.0)
    def _():
        tri = (lax.broadcasted_iota(i32, (LANES, LANES), 1)
               <= lax.broadcasted_iota(i32, (LANES, LANES), 0)).astype(bf16)

        def tie_body(j, run):
            k0 = pl.multiple_of(j * LANES, LANES)
            blk = key_ref[pl.ds(k0, LANES), :]
            tie = jnp.where(blk == thr_row, 1.0, 0.0)
            rank = run + _dot(tri, tie.astype(bf16))
            drop = (tie * jnp.where(rank > keep_n, 1.0, 0.0)) > 0.5
            key_ref[pl.ds(k0, LANES), :] = jnp.where(drop, INT_MIN, blk)
            return run + jnp.sum(tie, axis=0, keepdims=True)

        lax.fori_loop(0, n_kt * (tk // LANES), tie_body, jnp.zeros((1, tq), f32))

    m_ref[...] = jnp.full(m_ref.shape, M_INIT, f32)
    acc_ref[...] = jnp.zeros(acc_ref.shape, f32)
    qx_ref[:, :D_LAT] = q_ref[0].reshape(nrow, D_LAT)
    onehot = ((lax.broadcasted_iota(i32, (nrow, tq), 0) & (tq - 1))
              == lax.broadcasted_iota(i32, (nrow, tq), 1))
    qx_ref[:, D_LAT:] = jnp.where(onehot, 1.0, 0.0).astype(bf16)
    scale = D_LAT ** -0.5 * LOG2E

    def att_body(j, carry):
        k0 = pl.multiple_of(j * tk, tk)
        ce = c_ref[0, pl.ds(k0, tk), :]
        mask_t = jnp.where(key_ref[pl.ds(k0, tk), :] >= thr_row, 0.0, NEG_BIAS).astype(bf16)
        rhs = jnp.concatenate([ce[:, :D_LAT], mask_t], axis=1)
        for r0 in range(0, nrow, ATT_ROWS):
            ra = slice(r0, r0 + ATT_ROWS)
            lg_ref[ra, :] = _dot_nt(qx_ref[ra, :], rhs)
            for r1 in range(r0, r0 + ATT_ROWS, SOFTMAX_ROWS):
                rs = slice(r1, r1 + SOFTMAX_ROWS)
                lh = lg_ref[rs, :]
                m_prev = m_ref[rs, :]
                m_new = jnp.maximum(m_prev, jnp.max(lh, axis=1, keepdims=True))
                alpha_ref[rs, :] = jnp.exp2((m_prev - m_new) * scale)
                p_ref[rs, :] = jnp.exp2((lh - _lane_tile(m_new, tk)) * scale).astype(bf16)
                m_ref[rs, :] = m_new
            alpha = alpha_ref[ra, :]
            acc_ref[ra, :] = acc_ref[ra, :] * jnp.concatenate([alpha, alpha], axis=1) + _dot(p_ref[ra, :], ce)
        return carry

    lax.fori_loop(0, n_kt, att_body, 0)

    for hd in range(D_HEADS):
        rs = slice(hd * tq, (hd + 1) * tq)
        o_ref[0, :, hd * D_LAT:(hd + 1) * D_LAT] = (acc_ref[rs, :D_LAT] / acc_ref[rs, D_LAT:]).astype(bf16)


def _dsa_attn(q, qi, wit, c_ext, ki, topk):
    b, _, l, _ = q.shape
    tq = DSA_Q_TILE
    assert tq == LANES and l % DSA_K_TILE == 0 and DSA_K_TILE % DSA_SCORE_ROWS == 0
    assert DSA_SCORE_ROWS == WORD_BITS * SUBLANES
    return pl.pallas_call(
        functools.partial(_dsa_attn_kernel, topk=topk),
        grid=(b, l // tq),
        in_specs=[pl.BlockSpec((1, D_HEADS, tq, D_LAT), lambda i, j: (i, 0, j, 0)),
                  pl.BlockSpec((1, D_IDX_HEADS, tq, D_IDX_DIM), lambda i, j: (i, 0, j, 0)),
                  pl.BlockSpec((1, D_IDX_HEADS, tq), lambda i, j: (i, 0, j)),
                  pl.BlockSpec((1, l, 2 * D_LAT), lambda i, j: (i, 0, 0)),
                  pl.BlockSpec((1, l, D_IDX_DIM), lambda i, j: (i, 0, 0))],
        out_specs=pl.BlockSpec((1, tq, D_HEADS * D_LAT), lambda i, j: (i, j, 0)),
        out_shape=jax.ShapeDtypeStruct((b, l, D_HEADS * D_LAT), bf16),
        scratch_shapes=[pltpu.VMEM((l, tq), i32),
                        pltpu.VMEM((WORD_BITS, l // WORD_BITS, tq), i32),
                        pltpu.VMEM((D_HEADS * tq, D_LAT + tq), bf16),
                        pltpu.VMEM((D_HEADS * tq, DSA_K_TILE), f32),
                        pltpu.VMEM((D_HEADS * tq, DSA_K_TILE), bf16),
                        pltpu.VMEM((D_HEADS * tq, LANES), f32),
                        pltpu.VMEM((D_HEADS * tq, LANES), f32),
                        pltpu.VMEM((D_HEADS * tq, 2 * D_LAT), f32)],
        compiler_params=_params("arbitrary", "arbitrary"),
    )(q, qi, wit, c_ext, ki)


def _dsa_out_kernel(x_ref, o_ref, wuv_ref, wout_ref, y_ref, ov_ref):
    for hd in range(D_HEADS):
        ov_ref[:, hd * D_VDIM:(hd + 1) * D_VDIM] = _dot(o_ref[:, hd * D_LAT:(hd + 1) * D_LAT], wuv_ref[hd])
    y_ref[...] = x_ref[...] + _dot(ov_ref[...].astype(bf16), wout_ref[...])


def _dsa_out(x, o, w_uv, w_out):
    n, d = x.shape
    tm = min(TOKEN_TILE, n)
    return pl.pallas_call(
        _dsa_out_kernel,
        grid=(n // tm,),
        in_specs=[pl.BlockSpec((tm, d), lambda i: (i, 0)),
                  pl.BlockSpec((tm, D_HEADS * D_LAT), lambda i: (i, 0)),
                  _full(w_uv.shape), _full(w_out.shape)],
        out_specs=pl.BlockSpec((tm, d), lambda i: (i, 0)),
        out_shape=jax.ShapeDtypeStruct((n, d), f32),
        scratch_shapes=[pltpu.VMEM((tm, D_HEADS * D_VDIM), f32)],
        compiler_params=_params("parallel"),
    )(x, o, w_uv.astype(bf16), w_out.astype(bf16))


def _mixer_dsa(x, g, w_in, kv_g, w_uv, w_out):
    b, l, d = x.shape
    topk = min(D_TOPK_MAX, l // 4)
    q, c_ext, qi, ki, wit = _dsa_proj(x, g, w_in, kv_g)
    o = _dsa_attn(q, qi, wit, c_ext, ki, topk)
    return _dsa_out(x.reshape(b * l, d), o.reshape(b * l, D_HEADS * D_LAT), w_uv, w_out).reshape(b, l, d)


@jax.jit
def _trunk(x, norm_mix_g, norm_mlp_g, final_g, a_w_in, a_v_g, a_w_s, a_b_s, a_w_out, b_w_in, b_conv_w,
           b_w_out, c_w_in, c_w_grp, c_scale, d_w_in, d_kv_g, d_w_uv, d_w_out, mlp_w1, mlp_w2):
    b, l, d = x.shape
    depth = norm_mix_g.shape[0]
    n_mixers = 4
    for i in range(depth):
        m, j = i % n_mixers, i // n_mixers
        g = norm_mix_g[i]
        if m == 0:
            x = _mixer_sgu(x.reshape(b * l, d), g, a_w_in[j], a_v_g[j], a_w_s[j], a_b_s[j],
                           a_w_out[j]).reshape(b, l, d)
        elif m == 1:
            x = _mixer_conv(x, g, b_w_in[j], b_conv_w[j], b_w_out[j])
        elif m == 2:
            x = _mixer_pool(x, g, c_w_in[j], c_w_grp[j], c_scale[j])
        else:
            x = _mixer_dsa(x, g, d_w_in[j], d_kv_g[j], d_w_uv[j], d_w_out[j])
        x = _mlp(x.reshape(b * l, d), norm_mlp_g[i], mlp_w1[i], mlp_w2[i], final_g,
                 apply_final=(i == depth - 1)).reshape(b, l, d)
    return x


def kernel(x, norm_mix_g, norm_mlp_g, final_g, a_w_in, a_v_g, a_w_s, a_b_s, a_w_out, b_w_in, b_conv_w,
           b_w_out, c_w_in, c_w_grp, c_scale, d_w_in, d_kv_g, d_w_uv, d_w_out, mlp_w1, mlp_w2):
    return _trunk(x, norm_mix_g, norm_mlp_g, final_g, a_w_in, a_v_g, a_w_s, a_b_s, a_w_out, b_w_in,
                  b_conv_w, b_w_out, c_w_in, c_w_grp, c_scale, d_w_in, d_kv_g, d_w_uv, d_w_out,
                  mlp_w1, mlp_w2)
```

```python
import functools

import jax
import jax.numpy as jnp
from jax import lax
from jax.experimental import pallas as pl
from jax.experimental.pallas import tpu as pltpu

EPS = 1e-6
LANES = 128
VMEM_LIMIT_BYTES = 56 * 1024 * 1024

A_CHUNK = 128
A_GROUPS = 8
B_KSIZE = 3
C_WINDOWS = (2, 4, 8, 16)
D_HEADS = 16
D_LAT = 128
D_VDIM = 64
D_IDX_HEADS = 8
D_IDX_DIM = 64
D_TOPK_MAX = 256

INT_MIN = -(2 ** 31)
NEG_BIAS = -1e30
M_INIT = -1e29
LOG2E = 1.4426950408889634

TOKEN_TILE = 1024
DSA_Q_TILE = 128
DSA_K_TILE = 512
DSA_SCORE_ROWS = 256
ATT_ROWS = 512
SOFTMAX_ROWS = 32
SUBLANES = 8
WORD_BITS = 32

f32 = jnp.float32
bf16 = jnp.bfloat16
i32 = jnp.int32


def _dot(a, b):
    return jnp.dot(a, b, preferred_element_type=f32)


def _dot_nt(a, b):
    return lax.dot_general(a, b, (((1,), (1,)), ((), ())), preferred_element_type=f32)


def _rms(x, g):
    ms = jnp.mean(x * x, axis=-1, keepdims=True)
    return x * lax.rsqrt(ms + EPS) * g


def _params(*sem):
    return pltpu.CompilerParams(dimension_semantics=sem, vmem_limit_bytes=VMEM_LIMIT_BYTES)


def _full(shape):
    nd = len(shape)
    return pl.BlockSpec(shape, lambda *_: (0,) * nd)


def _mlp_kernel(x_ref, g_ref, w1_ref, w2_ref, fg_ref, o_ref, *, ff_chunk, apply_final):
    x = x_ref[...]
    h = _rms(x, g_ref[...]).astype(bf16)
    acc = x
    for c in range(w1_ref.shape[1] // ff_chunk):
        a = jnp.maximum(_dot(h, w1_ref[:, c * ff_chunk:(c + 1) * ff_chunk]), 0.0)
        acc = acc + _dot((a * a).astype(bf16), w2_ref[c * ff_chunk:(c + 1) * ff_chunk, :])
    if apply_final:
        acc = _rms(acc, fg_ref[...])
    o_ref[...] = acc


def _mlp(x, g, w1, w2, final_g, apply_final):
    n, d = x.shape
    dff = w1.shape[1]
    tm = min(TOKEN_TILE, n)
    row = pl.BlockSpec((tm, d), lambda i: (i, 0))
    return pl.pallas_call(
        functools.partial(_mlp_kernel, ff_chunk=min(1024, dff), apply_final=apply_final),
        grid=(n // tm,),
        in_specs=[row, _full((1, d)), _full((d, dff)), _full((dff, d)), _full((1, d))],
        out_specs=row,
        out_shape=jax.ShapeDtypeStruct((n, d), f32),
        compiler_params=_params("parallel"),
    )(x, g.reshape(1, d), w1.astype(bf16), w2.astype(bf16), final_g.reshape(1, d))


def _sgu_kernel(x_ref, g_ref, win_ref, vg_ref, ws_ref, bst_ref, wout_ref, o_ref, s_ref):
    tm, _ = x_ref.shape
    width = wout_ref.shape[0]
    gdim = width // A_GROUPS
    x = x_ref[...]
    h = _rms(x, g_ref[...]).astype(bf16)
    u = jax.nn.gelu(_dot(h, win_ref[:, :width]))
    v = jax.nn.gelu(_dot(h, win_ref[:, width:]))
    v = _rms(v, vg_ref[...]).astype(bf16)
    r = lax.broadcasted_iota(i32, (A_CHUNK, A_CHUNK), 0)
    c = lax.broadcasted_iota(i32, (A_CHUNK, A_CHUNK), 1)
    causal = c <= r
    bst = bst_ref[...]
    for gi in range(A_GROUPS):
        ws = jnp.where(causal, ws_ref[gi], 0.0).astype(bf16)
        bias = jnp.broadcast_to(bst[:, gi:gi + 1], (A_CHUNK, gdim))
        for n in range(tm // A_CHUNK):
            vb = v[n * A_CHUNK:(n + 1) * A_CHUNK, gi * gdim:(gi + 1) * gdim]
            s_ref[n * A_CHUNK:(n + 1) * A_CHUNK, gi * gdim:(gi + 1) * gdim] = _dot(ws, vb) + bias
    o_ref[...] = x + _dot((u * s_ref[...]).astype(bf16), wout_ref[...])


def _mixer_sgu(x, g, w_in, v_g, w_s, b_s, w_out):
    n, d = x.shape
    width = w_out.shape[0]
    tm = min(TOKEN_TILE, n)
    row = pl.BlockSpec((tm, d), lambda i: (i, 0))
    return pl.pallas_call(
        _sgu_kernel,
        grid=(n // tm,),
        in_specs=[row, _full((1, d)), _full((d, 2 * width)), _full((1, width)),
                  _full(w_s.shape), _full((A_CHUNK, A_GROUPS)), _full((width, d))],
        out_specs=row,
        out_shape=jax.ShapeDtypeStruct((n, d), f32),
        scratch_shapes=[pltpu.VMEM((tm, width), f32)],
        compiler_params=_params("parallel"),
    )(x, g.reshape(1, d), w_in.astype(bf16), v_g.reshape(1, width), w_s, b_s.T, w_out.astype(bf16))


CONV_HALO = 8


def _conv_kernel(x_ref, g_ref, win_ref, cw_ref, wout_ref, o_ref, z_ref):
    tm = x_ref.shape[1]
    width = wout_ref.shape[0]

    @pl.when(pl.program_id(1) == 0)
    def _():
        z_ref[0:CONV_HALO, :] = jnp.zeros((CONV_HALO, width), f32)

    x = x_ref[0]
    h = _rms(x, g_ref[...]).astype(bf16)
    bg = _dot(h, win_ref[:, :width])
    z = _dot(h, win_ref[:, width:2 * width]) * _dot(h, win_ref[:, 2 * width:])
    z_ref[CONV_HALO:CONV_HALO + tm, :] = z
    y = cw_ref[B_KSIZE - 1:B_KSIZE, :] * z
    for k in range(B_KSIZE - 1):
        shift = B_KSIZE - 1 - k
        y = y + cw_ref[k:k + 1, :] * z_ref[CONV_HALO - shift:CONV_HALO - shift + tm, :]
    z_ref[0:CONV_HALO, :] = z[tm - CONV_HALO:, :]
    o_ref[0] = x + _dot((bg * y).astype(bf16), wout_ref[...])


def _mixer_conv(x, g, w_in, conv_w, w_out):
    b, l, d = x.shape
    width = w_out.shape[0]
    tm = min(TOKEN_TILE, l)
    row = pl.BlockSpec((1, tm, d), lambda i, j: (i, j, 0))
    return pl.pallas_call(
        _conv_kernel,
        grid=(b, l // tm),
        in_specs=[row, _full((1, d)), _full((d, 3 * width)), _full((B_KSIZE, width)), _full((width, d))],
        out_specs=row,
        out_shape=jax.ShapeDtypeStruct((b, l, d), f32),
        scratch_shapes=[pltpu.VMEM((CONV_HALO + tm, width), f32)],
        compiler_params=_params("arbitrary", "arbitrary"),
    )(x, g.reshape(1, d), w_in.astype(bf16), conv_w.reshape(B_KSIZE, width), w_out.astype(bf16))


POOL_HALO = 16
POOL_PAD = 8


def _pool_kernel(x_ref, g_ref, win_ref, wgrp_ref, sc_ref, o_ref, carry_ref, pa_ref, pb_ref, y_ref):
    tm = x_ref.shape[1]
    d = x_ref.shape[2]
    gdim = d // len(C_WINDOWS)
    n_ext = tm + POOL_HALO
    j = pl.program_id(1)

    @pl.when(j == 0)
    def _():
        carry_ref[...] = jnp.zeros(carry_ref.shape, f32)

    x = x_ref[0]
    h = _rms(x, g_ref[...]).astype(bf16)
    z = _dot(h, win_ref[...])
    pa_ref[0:POOL_PAD, :] = jnp.zeros((POOL_PAD, gdim), f32)
    pb_ref[0:POOL_PAD, :] = jnp.zeros((POOL_PAD, gdim), f32)
    pos = j * tm + lax.broadcasted_iota(i32, (tm, 1), 0) + 1
    for gi, w in enumerate(C_WINDOWS):
        zg = z[:, gi * gdim:(gi + 1) * gdim]
        src, dst = pa_ref, pb_ref
        src[POOL_PAD:POOL_PAD + POOL_HALO, :] = carry_ref[:, gi * gdim:(gi + 1) * gdim]
        src[POOL_PAD + POOL_HALO:, :] = zg
        shift = 1
        while True:
            acc = src[POOL_PAD:, :] + src[POOL_PAD - shift:POOL_PAD - shift + n_ext, :]
            shift *= 2
            if shift == w:
                break
            dst[POOL_PAD:, :] = acc
            src, dst = dst, src
        cnt = jnp.minimum(pos, w).astype(f32)
        pooled = acc[POOL_HALO:, :] / cnt - zg
        y_ref[:, gi * gdim:(gi + 1) * gdim] = _dot(pooled.astype(bf16), wgrp_ref[gi])
    carry_ref[...] = z[tm - POOL_HALO:, :]
    o_ref[0] = x + y_ref[...] * sc_ref[...]


def _mixer_pool(x, g, w_in, w_grp, scale):
    b, l, d = x.shape
    gdim = d // len(C_WINDOWS)
    tm = min(TOKEN_TILE, l)
    row = pl.BlockSpec((1, tm, d), lambda i, j: (i, j, 0))
    ext = POOL_PAD + POOL_HALO + tm
    return pl.pallas_call(
        _pool_kernel,
        grid=(b, l // tm),
        in_specs=[row, _full((1, d)), _full((d, d)), _full(w_grp.shape), _full((1, d))],
        out_specs=row,
        out_shape=jax.ShapeDtypeStruct((b, l, d), f32),
        scratch_shapes=[pltpu.VMEM((POOL_HALO, d), f32), pltpu.VMEM((ext, gdim), f32),
                        pltpu.VMEM((ext, gdim), f32), pltpu.VMEM((tm, d), f32)],
        compiler_params=_params("arbitrary", "arbitrary"),
    )(x, g.reshape(1, d), w_in.astype(bf16), w_grp.astype(bf16), scale.reshape(1, d))


def _dsa_proj_kernel(x_ref, g_ref, wq_ref, wc_ref, wqi_ref, wkw_ref, kvg_ref,
                     q_ref, c_ref, qi_ref, ki_ref, wit_ref):
    tm = x_ref.shape[1]
    h = _rms(x_ref[0], g_ref[...]).astype(bf16)
    q = _dot(h, wq_ref[...])
    for hd in range(D_HEADS):
        q_ref[0, hd] = q[:, hd * D_LAT:(hd + 1) * D_LAT].astype(bf16)
    c = _rms(_dot(h, wc_ref[...]), kvg_ref[...]).astype(bf16)
    c_ref[0, :, :D_LAT] = c
    c_ref[0, :, D_LAT:] = jnp.ones((tm, D_LAT), bf16)
    qi = _dot(h, wqi_ref[...])
    for hd in range(D_IDX_HEADS):
        qi_ref[0, hd] = qi[:, hd * D_IDX_DIM:(hd + 1) * D_IDX_DIM].astype(bf16)
    kw = _dot(h, wkw_ref[...])
    ki_ref[0] = kw[:, :D_IDX_DIM].astype(bf16)
    wit_ref[0] = kw.T[D_IDX_DIM:D_IDX_DIM + D_IDX_HEADS, :] * (D_IDX_HEADS ** -0.5 * D_IDX_DIM ** -0.5)


def _dsa_proj(x, g, w_in, kv_g):
    b, l, d = x.shape
    nq = D_HEADS * D_LAT
    nqi = D_IDX_HEADS * D_IDX_DIM
    o_c, o_qi, o_ki = nq, nq + D_LAT, nq + D_LAT + nqi
    w_in = w_in.astype(bf16)
    w_kw = jnp.pad(w_in[:, o_ki:], ((0, 0), (0, LANES - (D_IDX_DIM + D_IDX_HEADS))))
    tm = min(TOKEN_TILE, l)
    row = pl.BlockSpec((1, tm, d), lambda i, j: (i, j, 0))
    return pl.pallas_call(
        _dsa_proj_kernel,
        grid=(b, l // tm),
        in_specs=[row, _full((1, d)), _full((d, nq)), _full((d, D_LAT)), _full((d, nqi)),
                  _full((d, LANES)), _full((1, D_LAT))],
        out_specs=[pl.BlockSpec((1, D_HEADS, tm, D_LAT), lambda i, j: (i, 0, j, 0)),
                   pl.BlockSpec((1, tm, 2 * D_LAT), lambda i, j: (i, j, 0)),
                   pl.BlockSpec((1, D_IDX_HEADS, tm, D_IDX_DIM), lambda i, j: (i, 0, j, 0)),
                   pl.BlockSpec((1, tm, D_IDX_DIM), lambda i, j: (i, j, 0)),
                   pl.BlockSpec((1, D_IDX_HEADS, tm), lambda i, j: (i, 0, j))],
        out_shape=[jax.ShapeDtypeStruct((b, D_HEADS, l, D_LAT), bf16),
                   jax.ShapeDtypeStruct((b, l, 2 * D_LAT), bf16),
                   jax.ShapeDtypeStruct((b, D_IDX_HEADS, l, D_IDX_DIM), bf16),
                   jax.ShapeDtypeStruct((b, l, D_IDX_DIM), bf16),
                   jax.ShapeDtypeStruct((b, D_IDX_HEADS, l), f32)],
        compiler_params=_params("parallel", "parallel"),
    )(x, g.reshape(1, d), w_in[:, :o_c], w_in[:, o_c:o_qi], w_in[:, o_qi:o_ki], w_kw,
      kv_g.reshape(1, D_LAT))


def _bit_transpose(words):
    a = list(reversed(words))
    j, mask = WORD_BITS // 2, 0x0000FFFF
    while j:
        k = 0
        while k < WORD_BITS:
            t = (a[k] ^ lax.shift_right_logical(a[k + j], jnp.int32(j))) & mask
            a[k] = a[k] ^ t
            a[k + j] = a[k + j] ^ (t << j)
            k = (k + j + 1) & ~j
        j >>= 1
        mask ^= mask << j
    return a


def _lane_tile(a, n):
    return a if n == LANES else jnp.concatenate([a] * (n // LANES), axis=1)


def _dsa_attn_kernel(q_ref, qi_ref, wit_ref, c_ref, ki_ref, o_ref,
                     key_ref, plane_ref, qx_ref, lg_ref, p_ref, m_ref, alpha_ref, acc_ref, *, topk):
    tq = q_ref.shape[2]
    tk = DSA_K_TILE
    ts = DSA_SCORE_ROWS
    nrow = D_HEADS * tq
    n_word = plane_ref.shape[1]
    t0 = pl.program_id(1) * tq
    n_blk = (t0 + tq + ts - 1) // ts
    kf = float(topk)

    @pl.when((pl.program_id(0) == 0) & (pl.program_id(1) == 0))
    def _():
        plane_ref[...] = jnp.zeros(plane_ref.shape, i32)

    qi = qi_ref[0].reshape(D_IDX_HEADS * tq, D_IDX_DIM)
    wit = wit_ref[0]
    key_pos = lax.broadcasted_iota(i32, (ts, tq), 0)
    qry_pos = t0 + lax.broadcasted_iota(i32, (ts, tq), 1)

    def score_block(b):
        k0 = pl.multiple_of(b * ts, ts)
        lg = _dot_nt(ki_ref[0, pl.ds(k0, ts), :], qi)
        s = None
        for hd in range(D_IDX_HEADS):
            term = jnp.maximum(lg[:, hd * tq:(hd + 1) * tq], 0.0) * wit[hd:hd + 1, :]
            s = term if s is None else s + term
        s = jnp.where(s == 0.0, 0.0, s)
        bits = pltpu.bitcast(s, i32)
        key = bits ^ ((bits >> 31) & 0x7FFFFFFF)
        key = jnp.where(k0 + key_pos <= qry_pos, key, INT_MIN)
        key_ref[pl.ds(k0, ts), :] = key
        unsigned = (key ^ INT_MIN).reshape(WORD_BITS, SUBLANES, tq)
        planes = _bit_transpose([unsigned[m] for m in range(WORD_BITS)])
        w0 = pl.multiple_of(b * SUBLANES, SUBLANES)
        for p in range(WORD_BITS):
            plane_ref[p, pl.ds(w0, SUBLANES), :] = planes[p]

    def score_pair(j, carry):
        score_block(2 * j)
        score_block(2 * j + 1)
        return carry

    lax.fori_loop(0, n_blk // 2, score_pair, 0)

    @pl.when(n_blk % 2 == 1)
    def _():
        score_block(n_blk - 1)

    word = lax.broadcasted_iota(i32, (n_word, tq), 0)
    first_row = (word >> 3) * ts + (word & (SUBLANES - 1))
    n_valid = ((t0 + lax.broadcasted_iota(i32, (n_word, tq), 1) - first_row) >> 3) + 1
    cand0 = jnp.where(n_valid >= WORD_BITS, -1,
                      lax.shift_left(jnp.int32(1), jnp.clip(n_valid, 0, WORD_BITS - 1)) - 1)

    def row_sum(v):
        parts = [v[r:r + SUBLANES] for r in range(0, n_word, SUBLANES)]
        while len(parts) > 1:
            parts = [a + b for a, b in zip(parts[::2], parts[1::2])] + parts[len(parts) & ~1:]
        return jnp.sum(parts[0].astype(f32), axis=0, keepdims=True)

    def plane_body(p, carry):
        cand, need, thr = carry
        ones = cand & plane_ref[p]
        n_ones = row_sum(lax.population_count(ones))
        take = n_ones >= need
        thr = jnp.where(take, thr | lax.shift_left(jnp.int32(1), WORD_BITS - 1 - p), thr)
        return jnp.where(take, ones, cand ^ ones), jnp.where(take, need, need - n_ones), thr

    cand, keep_n, thr = lax.fori_loop(
        0, WORD_BITS, plane_body, (cand0, jnp.full((1, tq), kf, f32), jnp.zeros((1, tq), i32)))
    n_tied = row_sum(lax.population_count(cand))
    few = t0 + lax.broadcasted_iota(i32, (1, tq), 1) + 1 < topk
    thr_row = jnp.where(few, INT_MIN + 1, thr ^ INT_MIN)

    @pl.when(jnp.max(jnp.where(few, 0.0, n_tied - keep_n)) > 0.0)
    def _():
        tri = (lax.broadcasted_iota(i32, (LANES, LANES), 1)
               <= lax.broadcasted_iota(i32, (LANES, LANES), 0)).astype(bf16)

        def tie_body(j, run):
            k0 = pl.multiple_of(j * LANES, LANES)
            blk = key_ref[pl.ds(k0, LANES), :]
            tie = jnp.where(blk == thr_row, 1.0, 0.0)
            rank = run + _dot(tri, tie.astype(bf16))
            drop = (tie * jnp.where(rank > keep_n, 1.0, 0.0)) > 0.5
            key_ref[pl.ds(k0, LANES), :] = jnp.where(drop, INT_MIN, blk)
            return run + jnp.sum(tie, axis=0, keepdims=True)

        lax.fori_loop(0, n_blk * (ts // LANES), tie_body, jnp.zeros((1, tq), f32))

    m_ref[...] = jnp.full(m_ref.shape, M_INIT, f32)
    acc_ref[...] = jnp.zeros(acc_ref.shape, f32)
    qx_ref[:, :D_LAT] = q_ref[0].reshape(nrow, D_LAT)
    onehot = ((lax.broadcasted_iota(i32, (nrow, tq), 0) & (tq - 1))
              == lax.broadcasted_iota(i32, (nrow, tq), 1))
    qx_ref[:, D_LAT:] = jnp.where(onehot, 1.0, 0.0).astype(bf16)
    scale = D_LAT ** -0.5 * LOG2E

    def att_tile(k0, width):
        ce = c_ref[0, pl.ds(k0, width), :]
        mask_t = jnp.where(key_ref[pl.ds(k0, width), :] >= thr_row, 0.0, NEG_BIAS).astype(bf16)
        rhs = jnp.concatenate([ce[:, :D_LAT], mask_t], axis=1)
        for r0 in range(0, nrow, ATT_ROWS):
            ra = slice(r0, r0 + ATT_ROWS)
            lg_ref[ra, :width] = _dot_nt(qx_ref[ra, :], rhs)
            for r1 in range(r0, r0 + ATT_ROWS, SOFTMAX_ROWS):
                rs = slice(r1, r1 + SOFTMAX_ROWS)
                lh = lg_ref[rs, :width]
                m_prev = m_ref[rs, :]
                m_new = jnp.maximum(m_prev, jnp.max(lh, axis=1, keepdims=True))
                alpha_ref[rs, :] = jnp.exp2((m_prev - m_new) * scale)
                p_ref[rs, :width] = jnp.exp2((lh - _lane_tile(m_new, width)) * scale).astype(bf16)
                m_ref[rs, :] = m_new
            alpha = alpha_ref[ra, :]
            acc_ref[ra, :] = (acc_ref[ra, :] * jnp.concatenate([alpha, alpha], axis=1)
                              + _dot(p_ref[ra, :width], ce))

    def att_body(j, carry):
        att_tile(pl.multiple_of(j * tk, tk), tk)
        return carry

    lax.fori_loop(0, n_blk // (tk // ts), att_body, 0)

    @pl.when(n_blk % (tk // ts) == 1)
    def _():
        att_tile(pl.multiple_of((n_blk - 1) * ts, ts), ts)

    for hd in range(D_HEADS):
        rs = slice(hd * tq, (hd + 1) * tq)
        o_ref[0, :, hd * D_LAT:(hd + 1) * D_LAT] = (acc_ref[rs, :D_LAT] / acc_ref[rs, D_LAT:]).astype(bf16)


def _dsa_attn(q, qi, wit, c_ext, ki, topk):
    b, _, l, _ = q.shape
    tq = DSA_Q_TILE
    assert tq == LANES and l % DSA_K_TILE == 0 and DSA_K_TILE == 2 * DSA_SCORE_ROWS
    assert DSA_SCORE_ROWS == WORD_BITS * SUBLANES
    return pl.pallas_call(
        functools.partial(_dsa_attn_kernel, topk=topk),
        grid=(b, l // tq),
        in_specs=[pl.BlockSpec((1, D_HEADS, tq, D_LAT), lambda i, j: (i, 0, j, 0)),
                  pl.BlockSpec((1, D_IDX_HEADS, tq, D_IDX_DIM), lambda i, j: (i, 0, j, 0)),
                  pl.BlockSpec((1, D_IDX_HEADS, tq), lambda i, j: (i, 0, j)),
                  pl.BlockSpec((1, l, 2 * D_LAT), lambda i, j: (i, 0, 0)),
                  pl.BlockSpec((1, l, D_IDX_DIM), lambda i, j: (i, 0, 0))],
        out_specs=pl.BlockSpec((1, tq, D_HEADS * D_LAT), lambda i, j: (i, j, 0)),
        out_shape=jax.ShapeDtypeStruct((b, l, D_HEADS * D_LAT), bf16),
        scratch_shapes=[pltpu.VMEM((l, tq), i32),
                        pltpu.VMEM((WORD_BITS, l // WORD_BITS, tq), i32),
                        pltpu.VMEM((D_HEADS * tq, D_LAT + tq), bf16),
                        pltpu.VMEM((D_HEADS * tq, DSA_K_TILE), f32),
                        pltpu.VMEM((D_HEADS * tq, DSA_K_TILE), bf16),
                        pltpu.VMEM((D_HEADS * tq, LANES), f32),
                        pltpu.VMEM((D_HEADS * tq, LANES), f32),
                        pltpu.VMEM((D_HEADS * tq, 2 * D_LAT), f32)],
        compiler_params=_params("arbitrary", "arbitrary"),
    )(q, qi, wit, c_ext, ki)


def _dsa_out_kernel(x_ref, o_ref, wuv_ref, wout_ref, y_ref, ov_ref):
    for hd in range(D_HEADS):
        ov_ref[:, hd * D_VDIM:(hd + 1) * D_VDIM] = _dot(o_ref[:, hd * D_LAT:(hd + 1) * D_LAT], wuv_ref[hd])
    y_ref[...] = x_ref[...] + _dot(ov_ref[...].astype(bf16), wout_ref[...])


def _dsa_out(x, o, w_uv, w_out):
    n, d = x.shape
    tm = min(TOKEN_TILE, n)
    return pl.pallas_call(
        _dsa_out_kernel,
        grid=(n // tm,),
        in_specs=[pl.BlockSpec((tm, d), lambda i: (i, 0)),
                  pl.BlockSpec((tm, D_HEADS * D_LAT), lambda i: (i, 0)),
                  _full(w_uv.shape), _full(w_out.shape)],
        out_specs=pl.BlockSpec((tm, d), lambda i: (i, 0)),
        out_shape=jax.ShapeDtypeStruct((n, d), f32),
        scratch_shapes=[pltpu.VMEM((tm, D_HEADS * D_VDIM), f32)],
        compiler_params=_params("parallel"),
    )(x, o, w_uv.astype(bf16), w_out.astype(bf16))


def _mixer_dsa(x, g, w_in, kv_g, w_uv, w_out):
    b, l, d = x.shape
    topk = min(D_TOPK_MAX, l // 4)
    q, c_ext, qi, ki, wit = _dsa_proj(x, g, w_in, kv_g)
    o = _dsa_attn(q, qi, wit, c_ext, ki, topk)
    return _dsa_out(x.reshape(b * l, d), o.reshape(b * l, D_HEADS * D_LAT), w_uv, w_out).reshape(b, l, d)


@jax.jit
def _trunk(x, norm_mix_g, norm_mlp_g, final_g, a_w_in, a_v_g, a_w_s, a_b_s, a_w_out, b_w_in, b_conv_w,
           b_w_out, c_w_in, c_w_grp, c_scale, d_w_in, d_kv_g, d_w_uv, d_w_out, mlp_w1, mlp_w2):
    b, l, d = x.shape
    depth = norm_mix_g.shape[0]
    n_mixers = 4
    for i in range(depth):
        m, j = i % n_mixers, i // n_mixers
        g = norm_mix_g[i]
        if m == 0:
            x = _mixer_sgu(x.reshape(b * l, d), g, a_w_in[j], a_v_g[j], a_w_s[j], a_b_s[j],
                           a_w_out[j]).reshape(b, l, d)
        elif m == 1:
            x = _mixer_conv(x, g, b_w_in[j], b_conv_w[j], b_w_out[j])
        elif m == 2:
            x = _mixer_pool(x, g, c_w_in[j], c_w_grp[j], c_scale[j])
        else:
            x = _mixer_dsa(x, g, d_w_in[j], d_kv_g[j], d_w_uv[j], d_w_out[j])
        x = _mlp(x.reshape(b * l, d), norm_mlp_g[i], mlp_w1[i], mlp_w2[i], final_g,
                 apply_final=(i == depth - 1)).reshape(b, l, d)
    return x


def kernel(x, norm_mix_g, norm_mlp_g, final_g, a_w_in, a_v_g, a_w_s, a_b_s, a_w_out, b_w_in, b_conv_w,
           b_w_out, c_w_in, c_w_grp, c_scale, d_w_in, d_kv_g, d_w_uv, d_w_out, mlp_w1, mlp_w2):
    return _trunk(x, norm_mix_g, norm_mlp_g, final_g, a_w_in, a_v_g, a_w_s, a_b_s, a_w_out, b_w_in,
                  b_conv_w, b_w_out, c_w_in, c_w_grp, c_scale, d_w_in, d_kv_g, d_w_uv, d_w_out,
                  mlp_w1, mlp_w2)
```

```python
import functools

import jax
import jax.numpy as jnp
from jax import lax
from jax.experimental import pallas as pl
from jax.experimental.pallas import tpu as pltpu

EPS = 1e-6
LANES = 128
VMEM_LIMIT_BYTES = 56 * 1024 * 1024

A_CHUNK = 128
A_GROUPS = 8
B_KSIZE = 3
C_WINDOWS = (2, 4, 8, 16)
D_HEADS = 16
D_LAT = 128
D_VDIM = 64
D_IDX_HEADS = 8
D_IDX_DIM = 64
D_TOPK_MAX = 256

INT_MIN = -(2 ** 31)
NEG_BIAS = -1e30
M_INIT = -1e29
LOG2E = 1.4426950408889634

TOKEN_TILE = 1024
DSA_Q_TILE = 128
DSA_K_TILE = 512
DSA_SCORE_ROWS = 256
ATT_COLS = 256
QK_AHEAD = 5
SOFTMAX_ROWS = 64
SUM_ROWS = 16
SUBLANES = 8
WORD_BITS = 32

f32 = jnp.float32
bf16 = jnp.bfloat16
i32 = jnp.int32


def _dot(a, b):
    return jnp.dot(a, b, preferred_element_type=f32)


def _dot_nt(a, b):
    return lax.dot_general(a, b, (((1,), (1,)), ((), ())), preferred_element_type=f32)


def _rms(x, g):
    ms = jnp.mean(x * x, axis=-1, keepdims=True)
    return x * lax.rsqrt(ms + EPS) * g


def _params(*sem):
    return pltpu.CompilerParams(dimension_semantics=sem, vmem_limit_bytes=VMEM_LIMIT_BYTES)


def _full(shape):
    nd = len(shape)
    return pl.BlockSpec(shape, lambda *_: (0,) * nd)


def _mlp_kernel(x_ref, g_ref, w1_ref, w2_ref, fg_ref, o_ref, *, ff_chunk, apply_final):
    x = x_ref[...]
    h = _rms(x, g_ref[...]).astype(bf16)
    acc = x
    for c in range(w1_ref.shape[1] // ff_chunk):
        a = jnp.maximum(_dot(h, w1_ref[:, c * ff_chunk:(c + 1) * ff_chunk]), 0.0)
        acc = acc + _dot((a * a).astype(bf16), w2_ref[c * ff_chunk:(c + 1) * ff_chunk, :])
    if apply_final:
        acc = _rms(acc, fg_ref[...])
    o_ref[...] = acc


def _mlp(x, g, w1, w2, final_g, apply_final):
    n, d = x.shape
    dff = w1.shape[1]
    tm = min(TOKEN_TILE, n)
    row = pl.BlockSpec((tm, d), lambda i: (i, 0))
    return pl.pallas_call(
        functools.partial(_mlp_kernel, ff_chunk=min(1024, dff), apply_final=apply_final),
        grid=(n // tm,),
        in_specs=[row, _full((1, d)), _full((d, dff)), _full((dff, d)), _full((1, d))],
        out_specs=row,
        out_shape=jax.ShapeDtypeStruct((n, d), f32),
        compiler_params=_params("parallel"),
    )(x, g.reshape(1, d), w1.astype(bf16), w2.astype(bf16), final_g.reshape(1, d))


def _sgu_kernel(x_ref, g_ref, win_ref, vg_ref, ws_ref, bst_ref, wout_ref, o_ref, s_ref):
    tm, _ = x_ref.shape
    width = wout_ref.shape[0]
    gdim = width // A_GROUPS
    x = x_ref[...]
    h = _rms(x, g_ref[...]).astype(bf16)
    u = jax.nn.gelu(_dot(h, win_ref[:, :width]))
    v = jax.nn.gelu(_dot(h, win_ref[:, width:]))
    v = _rms(v, vg_ref[...]).astype(bf16)
    r = lax.broadcasted_iota(i32, (A_CHUNK, A_CHUNK), 0)
    c = lax.broadcasted_iota(i32, (A_CHUNK, A_CHUNK), 1)
    causal = c <= r
    bst = bst_ref[...]
    for gi in range(A_GROUPS):
        ws = jnp.where(causal, ws_ref[gi], 0.0).astype(bf16)
        bias = jnp.broadcast_to(bst[:, gi:gi + 1], (A_CHUNK, gdim))
        for n in range(tm // A_CHUNK):
            vb = v[n * A_CHUNK:(n + 1) * A_CHUNK, gi * gdim:(gi + 1) * gdim]
            s_ref[n * A_CHUNK:(n + 1) * A_CHUNK, gi * gdim:(gi + 1) * gdim] = _dot(ws, vb) + bias
    o_ref[...] = x + _dot((u * s_ref[...]).astype(bf16), wout_ref[...])


def _mixer_sgu(x, g, w_in, v_g, w_s, b_s, w_out):
    n, d = x.shape
    width = w_out.shape[0]
    tm = min(TOKEN_TILE, n)
    row = pl.BlockSpec((tm, d), lambda i: (i, 0))
    return pl.pallas_call(
        _sgu_kernel,
        grid=(n // tm,),
        in_specs=[row, _full((1, d)), _full((d, 2 * width)), _full((1, width)),
                  _full(w_s.shape), _full((A_CHUNK, A_GROUPS)), _full((width, d))],
        out_specs=row,
        out_shape=jax.ShapeDtypeStruct((n, d), f32),
        scratch_shapes=[pltpu.VMEM((tm, width), f32)],
        compiler_params=_params("parallel"),
    )(x, g.reshape(1, d), w_in.astype(bf16), v_g.reshape(1, width), w_s, b_s.T, w_out.astype(bf16))


CONV_HALO = 8


def _conv_kernel(x_ref, g_ref, win_ref, cw_ref, wout_ref, o_ref, z_ref):
    tm = x_ref.shape[1]
    width = wout_ref.shape[0]

    @pl.when(pl.program_id(1) == 0)
    def _():
        z_ref[0:CONV_HALO, :] = jnp.zeros((CONV_HALO, width), f32)

    x = x_ref[0]
    h = _rms(x, g_ref[...]).astype(bf16)
    bg = _dot(h, win_ref[:, :width])
    z = _dot(h, win_ref[:, width:2 * width]) * _dot(h, win_ref[:, 2 * width:])
    z_ref[CONV_HALO:CONV_HALO + tm, :] = z
    y = cw_ref[B_KSIZE - 1:B_KSIZE, :] * z
    for k in range(B_KSIZE - 1):
        shift = B_KSIZE - 1 - k
        y = y + cw_ref[k:k + 1, :] * z_ref[CONV_HALO - shift:CONV_HALO - shift + tm, :]
    z_ref[0:CONV_HALO, :] = z[tm - CONV_HALO:, :]
    o_ref[0] = x + _dot((bg * y).astype(bf16), wout_ref[...])


def _mixer_conv(x, g, w_in, conv_w, w_out):
    b, l, d = x.shape
    width = w_out.shape[0]
    tm = min(TOKEN_TILE, l)
    row = pl.BlockSpec((1, tm, d), lambda i, j: (i, j, 0))
    return pl.pallas_call(
        _conv_kernel,
        grid=(b, l // tm),
        in_specs=[row, _full((1, d)), _full((d, 3 * width)), _full((B_KSIZE, width)), _full((width, d))],
        out_specs=row,
        out_shape=jax.ShapeDtypeStruct((b, l, d), f32),
        scratch_shapes=[pltpu.VMEM((CONV_HALO + tm, width), f32)],
        compiler_params=_params("arbitrary", "arbitrary"),
    )(x, g.reshape(1, d), w_in.astype(bf16), conv_w.reshape(B_KSIZE, width), w_out.astype(bf16))


POOL_HALO = 16
POOL_PAD = 8


def _pool_kernel(x_ref, g_ref, win_ref, wgrp_ref, sc_ref, o_ref, carry_ref, pa_ref, pb_ref, y_ref):
    tm = x_ref.shape[1]
    d = x_ref.shape[2]
    gdim = d // len(C_WINDOWS)
    n_ext = tm + POOL_HALO
    j = pl.program_id(1)

    @pl.when(j == 0)
    def _():
        carry_ref[...] = jnp.zeros(carry_ref.shape, f32)

    x = x_ref[0]
    h = _rms(x, g_ref[...]).astype(bf16)
    z = _dot(h, win_ref[...])
    pa_ref[0:POOL_PAD, :] = jnp.zeros((POOL_PAD, gdim), f32)
    pb_ref[0:POOL_PAD, :] = jnp.zeros((POOL_PAD, gdim), f32)
    pos = j * tm + lax.broadcasted_iota(i32, (tm, 1), 0) + 1
    for gi, w in enumerate(C_WINDOWS):
        zg = z[:, gi * gdim:(gi + 1) * gdim]
        src, dst = pa_ref, pb_ref
        src[POOL_PAD:POOL_PAD + POOL_HALO, :] = carry_ref[:, gi * gdim:(gi + 1) * gdim]
        src[POOL_PAD + POOL_HALO:, :] = zg
        shift = 1
        while True:
            acc = src[POOL_PAD:, :] + src[POOL_PAD - shift:POOL_PAD - shift + n_ext, :]
            shift *= 2
            if shift == w:
                break
            dst[POOL_PAD:, :] = acc
            src, dst = dst, src
        cnt = jnp.minimum(pos, w).astype(f32)
        pooled = acc[POOL_HALO:, :] / cnt - zg
        y_ref[:, gi * gdim:(gi + 1) * gdim] = _dot(pooled.astype(bf16), wgrp_ref[gi])
    carry_ref[...] = z[tm - POOL_HALO:, :]
    o_ref[0] = x + y_ref[...] * sc_ref[...]


def _mixer_pool(x, g, w_in, w_grp, scale):
    b, l, d = x.shape
    gdim = d // len(C_WINDOWS)
    tm = min(TOKEN_TILE, l)
    row = pl.BlockSpec((1, tm, d), lambda i, j: (i, j, 0))
    ext = POOL_PAD + POOL_HALO + tm
    return pl.pallas_call(
        _pool_kernel,
        grid=(b, l // tm),
        in_specs=[row, _full((1, d)), _full((d, d)), _full(w_grp.shape), _full((1, d))],
        out_specs=row,
        out_shape=jax.ShapeDtypeStruct((b, l, d), f32),
        scratch_shapes=[pltpu.VMEM((POOL_HALO, d), f32), pltpu.VMEM((ext, gdim), f32),
                        pltpu.VMEM((ext, gdim), f32), pltpu.VMEM((tm, d), f32)],
        compiler_params=_params("arbitrary", "arbitrary"),
    )(x, g.reshape(1, d), w_in.astype(bf16), w_grp.astype(bf16), scale.reshape(1, d))


def _dsa_proj_kernel(x_ref, g_ref, wq_ref, wc_ref, wqi_ref, wkw_ref, kvg_ref,
                     q_ref, c_ref, ct_ref, qi_ref, ki_ref, wit_ref):
    tm = x_ref.shape[1]
    h = _rms(x_ref[0], g_ref[...]).astype(bf16)
    q = _dot(h, wq_ref[...])
    for hd in range(D_HEADS):
        q_ref[0, hd] = q[:, hd * D_LAT:(hd + 1) * D_LAT].astype(bf16)
    c = _rms(_dot(h, wc_ref[...]), kvg_ref[...])
    c_ref[0] = c.astype(bf16)
    for blk in range(tm // DSA_SCORE_ROWS):
        rows = slice(blk * DSA_SCORE_ROWS, (blk + 1) * DSA_SCORE_ROWS)
        ct_ref[0, blk] = c[rows, :].T.astype(bf16)
    qi = _dot(h, wqi_ref[...])
    for hd in range(D_IDX_HEADS):
        qi_ref[0, hd] = qi[:, hd * D_IDX_DIM:(hd + 1) * D_IDX_DIM].astype(bf16)
    kw = _dot(h, wkw_ref[...])
    ki_ref[0] = kw[:, :D_IDX_DIM].astype(bf16)
    wit_ref[0] = kw.T[D_IDX_DIM:D_IDX_DIM + D_IDX_HEADS, :] * (D_IDX_HEADS ** -0.5 * D_IDX_DIM ** -0.5)


def _dsa_proj(x, g, w_in, kv_g):
    b, l, d = x.shape
    nq = D_HEADS * D_LAT
    nqi = D_IDX_HEADS * D_IDX_DIM
    o_c, o_qi, o_ki = nq, nq + D_LAT, nq + D_LAT + nqi
    w_in = w_in.astype(bf16)
    w_kw = jnp.pad(w_in[:, o_ki:], ((0, 0), (0, LANES - (D_IDX_DIM + D_IDX_HEADS))))
    tm = min(TOKEN_TILE, l)
    row = pl.BlockSpec((1, tm, d), lambda i, j: (i, j, 0))
    return pl.pallas_call(
        _dsa_proj_kernel,
        grid=(b, l // tm),
        in_specs=[row, _full((1, d)), _full((d, nq)), _full((d, D_LAT)), _full((d, nqi)),
                  _full((d, LANES)), _full((1, D_LAT))],
        out_specs=[pl.BlockSpec((1, D_HEADS, tm, D_LAT), lambda i, j: (i, 0, j, 0)),
                   pl.BlockSpec((1, tm, D_LAT), lambda i, j: (i, j, 0)),
                   pl.BlockSpec((1, tm // DSA_SCORE_ROWS, D_LAT, DSA_SCORE_ROWS), lambda i, j: (i, j, 0, 0)),
                   pl.BlockSpec((1, D_IDX_HEADS, tm, D_IDX_DIM), lambda i, j: (i, 0, j, 0)),
                   pl.BlockSpec((1, tm, D_IDX_DIM), lambda i, j: (i, j, 0)),
                   pl.BlockSpec((1, D_IDX_HEADS, tm), lambda i, j: (i, 0, j))],
        out_shape=[jax.ShapeDtypeStruct((b, D_HEADS, l, D_LAT), bf16),
                   jax.ShapeDtypeStruct((b, l, D_LAT), bf16),
                   jax.ShapeDtypeStruct((b, l // DSA_SCORE_ROWS, D_LAT, DSA_SCORE_ROWS), bf16),
                   jax.ShapeDtypeStruct((b, D_IDX_HEADS, l, D_IDX_DIM), bf16),
                   jax.ShapeDtypeStruct((b, l, D_IDX_DIM), bf16),
                   jax.ShapeDtypeStruct((b, D_IDX_HEADS, l), f32)],
        compiler_params=_params("parallel", "parallel"),
    )(x, g.reshape(1, d), w_in[:, :o_c], w_in[:, o_c:o_qi], w_in[:, o_qi:o_ki], w_kw,
      kv_g.reshape(1, D_LAT))


def _bit_transpose(words):
    a = list(reversed(words))
    j, mask = WORD_BITS // 2, 0x0000FFFF
    while j:
        k = 0
        while k < WORD_BITS:
            t = (a[k] ^ lax.shift_right_logical(a[k + j], jnp.int32(j))) & mask
            a[k] = a[k] ^ t
            a[k + j] = a[k + j] ^ (t << j)
            k = (k + j + 1) & ~j
        j >>= 1
        mask ^= mask << j
    return a


def _dsa_attn_kernel(q_ref, qi_ref, wit_ref, c_ref, ct_ref, ki_ref, o_ref,
                     key_ref, plane_ref, qx_ref, lg_ref, m_ref, acc_ref, *, topk):
    tq = q_ref.shape[2]
    tk = DSA_K_TILE
    ts = DSA_SCORE_ROWS
    nrow = D_HEADS * tq
    n_word = plane_ref.shape[1]
    t0 = pl.program_id(1) * tq
    n_blk = (t0 + tq + ts - 1) // ts
    kf = float(topk)

    @pl.when((pl.program_id(0) == 0) & (pl.program_id(1) == 0))
    def _():
        plane_ref[...] = jnp.zeros(plane_ref.shape, i32)

    qi = qi_ref[0].reshape(D_IDX_HEADS * tq, D_IDX_DIM)
    wit = wit_ref[0]
    key_pos = lax.broadcasted_iota(i32, (ts, tq), 0)
    qry_pos = t0 + lax.broadcasted_iota(i32, (ts, tq), 1)

    def score_block(b):
        k0 = pl.multiple_of(b * ts, ts)
        lg = _dot_nt(ki_ref[0, pl.ds(k0, ts), :], qi)
        s = None
        for hd in range(D_IDX_HEADS):
            term = jnp.maximum(lg[:, hd * tq:(hd + 1) * tq], 0.0) * wit[hd:hd + 1, :]
            s = term if s is None else s + term
        s = jnp.where(s == 0.0, 0.0, s)
        bits = pltpu.bitcast(s, i32)
        key = bits ^ ((bits >> 31) & 0x7FFFFFFF)
        key = jnp.where(k0 + key_pos <= qry_pos, key, INT_MIN)
        key_ref[pl.ds(k0, ts), :] = key
        unsigned = (key ^ INT_MIN).reshape(WORD_BITS, SUBLANES, tq)
        planes = _bit_transpose([unsigned[m] for m in range(WORD_BITS)])
        w0 = pl.multiple_of(b * SUBLANES, SUBLANES)
        for p in range(WORD_BITS):
            plane_ref[p, pl.ds(w0, SUBLANES), :] = planes[p]

    def score_pair(j, carry):
        score_block(2 * j)
        score_block(2 * j + 1)
        return carry

    lax.fori_loop(0, n_blk // 2, score_pair, 0)

    @pl.when(n_blk % 2 == 1)
    def _():
        score_block(n_blk - 1)

    word = lax.broadcasted_iota(i32, (n_word, tq), 0)
    first_row = (word >> 3) * ts + (word & (SUBLANES - 1))
    n_valid = ((t0 + lax.broadcasted_iota(i32, (n_word, tq), 1) - first_row) >> 3) + 1
    cand0 = jnp.where(n_valid >= WORD_BITS, -1,
                      lax.shift_left(jnp.int32(1), jnp.clip(n_valid, 0, WORD_BITS - 1)) - 1)

    def row_sum(v):
        parts = [v[r:r + SUBLANES] for r in range(0, n_word, SUBLANES)]
        while len(parts) > 1:
            parts = [a + b for a, b in zip(parts[::2], parts[1::2])] + parts[len(parts) & ~1:]
        return jnp.sum(parts[0].astype(f32), axis=0, keepdims=True)

    def plane_body(p, carry):
        cand, need, thr = carry
        ones = cand & plane_ref[p]
        n_ones = row_sum(lax.population_count(ones))
        take = n_ones >= need
        thr = jnp.where(take, thr | lax.shift_left(jnp.int32(1), WORD_BITS - 1 - p), thr)
        return jnp.where(take, ones, cand ^ ones), jnp.where(take, need, need - n_ones), thr

    cand, keep_n, thr = lax.fori_loop(
        0, WORD_BITS, plane_body, (cand0, jnp.full((1, tq), kf, f32), jnp.zeros((1, tq), i32)))
    n_tied = row_sum(lax.population_count(cand))
    few = t0 + lax.broadcasted_iota(i32, (1, tq), 1) + 1 < topk
    thr_row = jnp.where(few, INT_MIN + 1, thr ^ INT_MIN)

    @pl.when(jnp.max(jnp.where(few, 0.0, n_tied - keep_n)) > 0.0)
    def _():
        tri = (lax.broadcasted_iota(i32, (LANES, LANES), 1)
               <= lax.broadcasted_iota(i32, (LANES, LANES), 0)).astype(bf16)

        def tie_body(j, run):
            k0 = pl.multiple_of(j * LANES, LANES)
            blk = key_ref[pl.ds(k0, LANES), :]
            tie = jnp.where(blk == thr_row, 1.0, 0.0)
            rank = run + _dot(tri, tie.astype(bf16))
            drop = (tie * jnp.where(rank > keep_n, 1.0, 0.0)) > 0.5
            key_ref[pl.ds(k0, LANES), :] = jnp.where(drop, INT_MIN, blk)
            return run + jnp.sum(tie, axis=0, keepdims=True)

        lax.fori_loop(0, n_blk * (ts // LANES), tie_body, jnp.zeros((1, tq), f32))

    n_cb = nrow // ATT_COLS
    heads_per_cb = ATT_COLS // tq
    m_ref[...] = jnp.full(m_ref.shape, M_INIT, f32)
    acc_ref[...] = jnp.zeros(acc_ref.shape, f32)
    onehot = (lax.broadcasted_iota(i32, (tq, ATT_COLS), 0)
              == (lax.broadcasted_iota(i32, (tq, ATT_COLS), 1) & (tq - 1)))
    onehot = jnp.where(onehot, 1.0, 0.0).astype(bf16)
    for cb in range(n_cb):
        for i in range(heads_per_cb):
            qx_ref[cb, :D_LAT, i * tq:(i + 1) * tq] = q_ref[0, cb * heads_per_cb + i].astype(f32).T.astype(bf16)
        qx_ref[cb, D_LAT:, :] = onehot
    scale = D_LAT ** -0.5 * LOG2E

    def col_max(v):
        parts = [v[r:r + SUBLANES] for r in range(0, v.shape[0], SUBLANES)]
        while len(parts) > 1:
            parts = [jnp.maximum(a, b) for a, b in zip(parts[::2], parts[1::2])] + parts[len(parts) & ~1:]
        return jnp.max(parts[0], axis=0, keepdims=True)

    def att_tile(blk0, n_sub):
        width = n_sub * ts
        k0 = pl.multiple_of(blk0 * ts, ts)
        mask_t = jnp.where(key_ref[pl.ds(k0, width), :] >= thr_row, 0.0, NEG_BIAS).astype(bf16)
        keys = jnp.concatenate([c_ref[0, pl.ds(k0, width), :], mask_t], axis=1)
        vals_t = jnp.concatenate([ct_ref[0, blk0 + i] for i in range(n_sub)], axis=1)
        vals_t = jnp.concatenate([vals_t, jnp.ones((SUM_ROWS, width), bf16)], axis=0)
        for cb in range(min(QK_AHEAD, n_cb)):
            lg_ref[cb, :width, :] = _dot(keys, qx_ref[cb])
        for cb in range(n_cb):
            if cb + QK_AHEAD < n_cb:
                lg_ref[cb + QK_AHEAD, :width, :] = _dot(keys, qx_ref[cb + QK_AHEAD])
            m_prev = m_ref[cb]
            m_new = m_prev
            for r0 in range(0, width, SOFTMAX_ROWS):
                m_new = jnp.maximum(m_new, col_max(lg_ref[cb, r0:r0 + SOFTMAX_ROWS, :]))
            alpha = jnp.exp2((m_prev - m_new) * scale)
            p = jnp.concatenate(
                [jnp.exp2((lg_ref[cb, r0:r0 + SOFTMAX_ROWS, :] - m_new) * scale).astype(bf16)
                 for r0 in range(0, width, SOFTMAX_ROWS)], axis=0)
            m_ref[cb] = m_new
            acc_ref[cb] = acc_ref[cb] * alpha + _dot(vals_t, p)

    def att_body(j, carry):
        att_tile(j * (tk // ts), tk // ts)
        return carry

    lax.fori_loop(0, n_blk // (tk // ts), att_body, 0)

    @pl.when(n_blk % (tk // ts) == 1)
    def _():
        att_tile(n_blk - 1, 1)

    for cb in range(n_cb):
        out_t = acc_ref[cb, :D_LAT, :] / acc_ref[cb, D_LAT:D_LAT + 1, :]
        for i in range(heads_per_cb):
            hd = cb * heads_per_cb + i
            o_ref[0, :, hd * D_LAT:(hd + 1) * D_LAT] = out_t[:, i * tq:(i + 1) * tq].T.astype(bf16)


def _dsa_attn(q, qi, wit, c, ct, ki, topk):
    b, _, l, _ = q.shape
    tq = DSA_Q_TILE
    assert tq == LANES and l % DSA_K_TILE == 0 and DSA_K_TILE == 2 * DSA_SCORE_ROWS
    assert DSA_SCORE_ROWS == WORD_BITS * SUBLANES
    n_cb = D_HEADS * tq // ATT_COLS
    return pl.pallas_call(
        functools.partial(_dsa_attn_kernel, topk=topk),
        grid=(b, l // tq),
        in_specs=[pl.BlockSpec((1, D_HEADS, tq, D_LAT), lambda i, j: (i, 0, j, 0)),
                  pl.BlockSpec((1, D_IDX_HEADS, tq, D_IDX_DIM), lambda i, j: (i, 0, j, 0)),
                  pl.BlockSpec((1, D_IDX_HEADS, tq), lambda i, j: (i, 0, j)),
                  pl.BlockSpec((1, l, D_LAT), lambda i, j: (i, 0, 0)),
                  pl.BlockSpec((1, l // DSA_SCORE_ROWS, D_LAT, DSA_SCORE_ROWS), lambda i, j: (i, 0, 0, 0)),
                  pl.BlockSpec((1, l, D_IDX_DIM), lambda i, j: (i, 0, 0))],
        out_specs=pl.BlockSpec((1, tq, D_HEADS * D_LAT), lambda i, j: (i, j, 0)),
        out_shape=jax.ShapeDtypeStruct((b, l, D_HEADS * D_LAT), bf16),
        scratch_shapes=[pltpu.VMEM((l, tq), i32),
                        pltpu.VMEM((WORD_BITS, l // WORD_BITS, tq), i32),
                        pltpu.VMEM((n_cb, D_LAT + tq, ATT_COLS), bf16),
                        pltpu.VMEM((n_cb, DSA_K_TILE, ATT_COLS), f32),
                        pltpu.VMEM((n_cb, 1, ATT_COLS), f32),
                        pltpu.VMEM((n_cb, D_LAT + SUM_ROWS, ATT_COLS), f32)],
        compiler_params=_params("arbitrary", "arbitrary"),
    )(q, qi, wit, c, ct, ki)


def _dsa_out_kernel(x_ref, o_ref, wuv_ref, wout_ref, y_ref, ov_ref):
    for hd in range(D_HEADS):
        ov_ref[:, hd * D_VDIM:(hd + 1) * D_VDIM] = _dot(o_ref[:, hd * D_LAT:(hd + 1) * D_LAT], wuv_ref[hd])
    y_ref[...] = x_ref[...] + _dot(ov_ref[...].astype(bf16), wout_ref[...])


def _dsa_out(x, o, w_uv, w_out):
    n, d = x.shape
    tm = min(TOKEN_TILE, n)
    return pl.pallas_call(
        _dsa_out_kernel,
        grid=(n // tm,),
        in_specs=[pl.BlockSpec((tm, d), lambda i: (i, 0)),
                  pl.BlockSpec((tm, D_HEADS * D_LAT), lambda i: (i, 0)),
                  _full(w_uv.shape), _full(w_out.shape)],
        out_specs=pl.BlockSpec((tm, d), lambda i: (i, 0)),
        out_shape=jax.ShapeDtypeStruct((n, d), f32),
        scratch_shapes=[pltpu.VMEM((tm, D_HEADS * D_VDIM), f32)],
        compiler_params=_params("parallel"),
    )(x, o, w_uv.astype(bf16), w_out.astype(bf16))


def _mixer_dsa(x, g, w_in, kv_g, w_uv, w_out):
    b, l, d = x.shape
    topk = min(D_TOPK_MAX, l // 4)
    q, c, ct, qi, ki, wit = _dsa_proj(x, g, w_in, kv_g)
    o = _dsa_attn(q, qi, wit, c, ct, ki, topk)
    return _dsa_out(x.reshape(b * l, d), o.reshape(b * l, D_HEADS * D_LAT), w_uv, w_out).reshape(b, l, d)


@jax.jit
def _trunk(x, norm_mix_g, norm_mlp_g, final_g, a_w_in, a_v_g, a_w_s, a_b_s, a_w_out, b_w_in, b_conv_w,
           b_w_out, c_w_in, c_w_grp, c_scale, d_w_in, d_kv_g, d_w_uv, d_w_out, mlp_w1, mlp_w2):
    b, l, d = x.shape
    depth = norm_mix_g.shape[0]
    n_mixers = 4
    for i in range(depth):
        m, j = i % n_mixers, i // n_mixers
        g = norm_mix_g[i]
        if m == 0:
            x = _mixer_sgu(x.reshape(b * l, d), g, a_w_in[j], a_v_g[j], a_w_s[j], a_b_s[j],
                           a_w_out[j]).reshape(b, l, d)
        elif m == 1:
            x = _mixer_conv(x, g, b_w_in[j], b_conv_w[j], b_w_out[j])
        elif m == 2:
            x = _mixer_pool(x, g, c_w_in[j], c_w_grp[j], c_scale[j])
        else:
            x = _mixer_dsa(x, g, d_w_in[j], d_kv_g[j], d_w_uv[j], d_w_out[j])
        x = _mlp(x.reshape(b * l, d), norm_mlp_g[i], mlp_w1[i], mlp_w2[i], final_g,
                 apply_final=(i == depth - 1)).reshape(b, l, d)
    return x


def kernel(x, norm_mix_g, norm_mlp_g, final_g, a_w_in, a_v_g, a_w_s, a_b_s, a_w_out, b_w_in, b_conv_w,
           b_w_out, c_w_in, c_w_grp, c_scale, d_w_in, d_kv_g, d_w_uv, d_w_out, mlp_w1, mlp_w2):
    return _trunk(x, norm_mix_g, norm_mlp_g, final_g, a_w_in, a_v_g, a_w_s, a_b_s, a_w_out, b_w_in,
                  b_conv_w, b_w_out, c_w_in, c_w_grp, c_scale, d_w_in, d_kv_g, d_w_uv, d_w_out,
                  mlp_w1, mlp_w2)
```

```python
import functools

import jax
import jax.numpy as jnp
from jax import lax
from jax.experimental import pallas as pl
from jax.experimental.pallas import tpu as pltpu

EPS = 1e-6
LANES = 128
VMEM_LIMIT_BYTES = 56 * 1024 * 1024

A_CHUNK = 128
A_GROUPS = 8
B_KSIZE = 3
C_WINDOWS = (2, 4, 8, 16)
D_HEADS = 16
D_LAT = 128
D_VDIM = 64
D_IDX_HEADS = 8
D_IDX_DIM = 64
D_TOPK_MAX = 256

INT_MIN = -(2 ** 31)
NEG_BIAS = -1e30
M_INIT = -1e29
LOG2E = 1.4426950408889634

TOKEN_TILE = 1024
DSA_Q_TILE = 128
DSA_K_TILE = 512
DSA_SCORE_ROWS = 256
ATT_COLS = 256
QK_AHEAD = 5
SOFTMAX_ROWS = 64
SUM_ROWS = 16
SUBLANES = 8
WORD_BITS = 32

f32 = jnp.float32
bf16 = jnp.bfloat16
i32 = jnp.int32


def _dot(a, b):
    return jnp.dot(a, b, preferred_element_type=f32)


def _dot_nt(a, b):
    return lax.dot_general(a, b, (((1,), (1,)), ((), ())), preferred_element_type=f32)


def _rms(x, g):
    ms = jnp.mean(x * x, axis=-1, keepdims=True)
    return x * lax.rsqrt(ms + EPS) * g


def _params(*sem):
    return pltpu.CompilerParams(dimension_semantics=sem, vmem_limit_bytes=VMEM_LIMIT_BYTES)


def _full(shape):
    nd = len(shape)
    return pl.BlockSpec(shape, lambda *_: (0,) * nd)


def _mlp_kernel(x_ref, g_ref, w1_ref, w2_ref, fg_ref, o_ref, *, ff_chunk, apply_final):
    x = x_ref[...]
    h = _rms(x, g_ref[...]).astype(bf16)
    acc = x
    for c in range(w1_ref.shape[1] // ff_chunk):
        a = jnp.maximum(_dot(h, w1_ref[:, c * ff_chunk:(c + 1) * ff_chunk]), 0.0)
        acc = acc + _dot((a * a).astype(bf16), w2_ref[c * ff_chunk:(c + 1) * ff_chunk, :])
    if apply_final:
        acc = _rms(acc, fg_ref[...])
    o_ref[...] = acc


def _mlp(x, g, w1, w2, final_g, apply_final):
    n, d = x.shape
    dff = w1.shape[1]
    tm = min(TOKEN_TILE, n)
    row = pl.BlockSpec((tm, d), lambda i: (i, 0))
    return pl.pallas_call(
        functools.partial(_mlp_kernel, ff_chunk=min(1024, dff), apply_final=apply_final),
        grid=(n // tm,),
        in_specs=[row, _full((1, d)), _full((d, dff)), _full((dff, d)), _full((1, d))],
        out_specs=row,
        out_shape=jax.ShapeDtypeStruct((n, d), f32),
        compiler_params=_params("parallel"),
    )(x, g.reshape(1, d), w1.astype(bf16), w2.astype(bf16), final_g.reshape(1, d))


def _sgu_kernel(x_ref, g_ref, win_ref, vg_ref, ws_ref, bst_ref, wout_ref, o_ref, s_ref):
    tm, _ = x_ref.shape
    width = wout_ref.shape[0]
    gdim = width // A_GROUPS
    x = x_ref[...]
    h = _rms(x, g_ref[...]).astype(bf16)
    u = jax.nn.gelu(_dot(h, win_ref[:, :width]))
    v = jax.nn.gelu(_dot(h, win_ref[:, width:]))
    v = _rms(v, vg_ref[...]).astype(bf16)
    r = lax.broadcasted_iota(i32, (A_CHUNK, A_CHUNK), 0)
    c = lax.broadcasted_iota(i32, (A_CHUNK, A_CHUNK), 1)
    causal = c <= r
    bst = bst_ref[...]
    for gi in range(A_GROUPS):
        ws = jnp.where(causal, ws_ref[gi], 0.0).astype(bf16)
        bias = jnp.broadcast_to(bst[:, gi:gi + 1], (A_CHUNK, gdim))
        for n in range(tm // A_CHUNK):
            vb = v[n * A_CHUNK:(n + 1) * A_CHUNK, gi * gdim:(gi + 1) * gdim]
            s_ref[n * A_CHUNK:(n + 1) * A_CHUNK, gi * gdim:(gi + 1) * gdim] = _dot(ws, vb) + bias
    o_ref[...] = x + _dot((u * s_ref[...]).astype(bf16), wout_ref[...])


def _mixer_sgu(x, g, w_in, v_g, w_s, b_s, w_out):
    n, d = x.shape
    width = w_out.shape[0]
    tm = min(TOKEN_TILE, n)
    row = pl.BlockSpec((tm, d), lambda i: (i, 0))
    return pl.pallas_call(
        _sgu_kernel,
        grid=(n // tm,),
        in_specs=[row, _full((1, d)), _full((d, 2 * width)), _full((1, width)),
                  _full(w_s.shape), _full((A_CHUNK, A_GROUPS)), _full((width, d))],
        out_specs=row,
        out_shape=jax.ShapeDtypeStruct((n, d), f32),
        scratch_shapes=[pltpu.VMEM((tm, width), f32)],
        compiler_params=_params("parallel"),
    )(x, g.reshape(1, d), w_in.astype(bf16), v_g.reshape(1, width), w_s, b_s.T, w_out.astype(bf16))


CONV_HALO = 8


def _conv_kernel(x_ref, g_ref, win_ref, cw_ref, wout_ref, o_ref, z_ref):
    tm = x_ref.shape[1]
    width = wout_ref.shape[0]

    @pl.when(pl.program_id(1) == 0)
    def _():
        z_ref[0:CONV_HALO, :] = jnp.zeros((CONV_HALO, width), f32)

    x = x_ref[0]
    h = _rms(x, g_ref[...]).astype(bf16)
    bg = _dot(h, win_ref[:, :width])
    z = _dot(h, win_ref[:, width:2 * width]) * _dot(h, win_ref[:, 2 * width:])
    z_ref[CONV_HALO:CONV_HALO + tm, :] = z
    y = cw_ref[B_KSIZE - 1:B_KSIZE, :] * z
    for k in range(B_KSIZE - 1):
        shift = B_KSIZE - 1 - k
        y = y + cw_ref[k:k + 1, :] * z_ref[CONV_HALO - shift:CONV_HALO - shift + tm, :]
    z_ref[0:CONV_HALO, :] = z[tm - CONV_HALO:, :]
    o_ref[0] = x + _dot((bg * y).astype(bf16), wout_ref[...])


def _mixer_conv(x, g, w_in, conv_w, w_out):
    b, l, d = x.shape
    width = w_out.shape[0]
    tm = min(TOKEN_TILE, l)
    row = pl.BlockSpec((1, tm, d), lambda i, j: (i, j, 0))
    return pl.pallas_call(
        _conv_kernel,
        grid=(b, l // tm),
        in_specs=[row, _full((1, d)), _full((d, 3 * width)), _full((B_KSIZE, width)), _full((width, d))],
        out_specs=row,
        out_shape=jax.ShapeDtypeStruct((b, l, d), f32),
        scratch_shapes=[pltpu.VMEM((CONV_HALO + tm, width), f32)],
        compiler_params=_params("arbitrary", "arbitrary"),
    )(x, g.reshape(1, d), w_in.astype(bf16), conv_w.reshape(B_KSIZE, width), w_out.astype(bf16))


POOL_HALO = 16
POOL_PAD = 8


def _pool_kernel(x_ref, g_ref, win_ref, wgrp_ref, sc_ref, o_ref, carry_ref, pa_ref, pb_ref, y_ref):
    tm = x_ref.shape[1]
    d = x_ref.shape[2]
    gdim = d // len(C_WINDOWS)
    n_ext = tm + POOL_HALO
    j = pl.program_id(1)

    @pl.when(j == 0)
    def _():
        carry_ref[...] = jnp.zeros(carry_ref.shape, f32)

    x = x_ref[0]
    h = _rms(x, g_ref[...]).astype(bf16)
    z = _dot(h, win_ref[...])
    pa_ref[0:POOL_PAD, :] = jnp.zeros((POOL_PAD, gdim), f32)
    pb_ref[0:POOL_PAD, :] = jnp.zeros((POOL_PAD, gdim), f32)
    pos = j * tm + lax.broadcasted_iota(i32, (tm, 1), 0) + 1
    for gi, w in enumerate(C_WINDOWS):
        zg = z[:, gi * gdim:(gi + 1) * gdim]
        src, dst = pa_ref, pb_ref
        src[POOL_PAD:POOL_PAD + POOL_HALO, :] = carry_ref[:, gi * gdim:(gi + 1) * gdim]
        src[POOL_PAD + POOL_HALO:, :] = zg
        shift = 1
        while True:
            acc = src[POOL_PAD:, :] + src[POOL_PAD - shift:POOL_PAD - shift + n_ext, :]
            shift *= 2
            if shift == w:
                break
            dst[POOL_PAD:, :] = acc
            src, dst = dst, src
        cnt = jnp.minimum(pos, w).astype(f32)
        pooled = acc[POOL_HALO:, :] / cnt - zg
        y_ref[:, gi * gdim:(gi + 1) * gdim] = _dot(pooled.astype(bf16), wgrp_ref[gi])
    carry_ref[...] = z[tm - POOL_HALO:, :]
    o_ref[0] = x + y_ref[...] * sc_ref[...]


def _mixer_pool(x, g, w_in, w_grp, scale):
    b, l, d = x.shape
    gdim = d // len(C_WINDOWS)
    tm = min(TOKEN_TILE, l)
    row = pl.BlockSpec((1, tm, d), lambda i, j: (i, j, 0))
    ext = POOL_PAD + POOL_HALO + tm
    return pl.pallas_call(
        _pool_kernel,
        grid=(b, l // tm),
        in_specs=[row, _full((1, d)), _full((d, d)), _full(w_grp.shape), _full((1, d))],
        out_specs=row,
        out_shape=jax.ShapeDtypeStruct((b, l, d), f32),
        scratch_shapes=[pltpu.VMEM((POOL_HALO, d), f32), pltpu.VMEM((ext, gdim), f32),
                        pltpu.VMEM((ext, gdim), f32), pltpu.VMEM((tm, d), f32)],
        compiler_params=_params("arbitrary", "arbitrary"),
    )(x, g.reshape(1, d), w_in.astype(bf16), w_grp.astype(bf16), scale.reshape(1, d))


def _dsa_proj_kernel(x_ref, g_ref, wq_ref, wckw_ref, wqi_ref, kvg_ref,
                     q_ref, c_ref, ct_ref, qi_ref, ki_ref, wit_ref):
    tm = x_ref.shape[1]
    h = _rms(x_ref[0], g_ref[...]).astype(bf16)
    q = _dot(h, wq_ref[...])
    for hd in range(D_HEADS):
        q_ref[0, hd] = q[:, hd * D_LAT:(hd + 1) * D_LAT].astype(bf16)
    ckw = _dot(h, wckw_ref[...])
    c = _rms(ckw[:, :D_LAT], kvg_ref[...])
    kw = ckw[:, D_LAT:]
    c_ref[0] = c.astype(bf16)
    for blk in range(tm // DSA_SCORE_ROWS):
        rows = slice(blk * DSA_SCORE_ROWS, (blk + 1) * DSA_SCORE_ROWS)
        ct_ref[0, blk] = c[rows, :].T.astype(bf16)
    qi = _dot(h, wqi_ref[...])
    for hd in range(D_IDX_HEADS):
        qi_ref[0, hd] = qi[:, hd * D_IDX_DIM:(hd + 1) * D_IDX_DIM].astype(bf16)
    ki_ref[0] = kw[:, :D_IDX_DIM].astype(bf16)
    wit_ref[0] = kw.T[D_IDX_DIM:D_IDX_DIM + D_IDX_HEADS, :] * (D_IDX_HEADS ** -0.5 * D_IDX_DIM ** -0.5)


def _dsa_proj(x, g, w_in, kv_g):
    b, l, d = x.shape
    nq = D_HEADS * D_LAT
    nqi = D_IDX_HEADS * D_IDX_DIM
    o_c, o_qi, o_ki = nq, nq + D_LAT, nq + D_LAT + nqi
    w_in = w_in.astype(bf16)
    w_ckw = jnp.concatenate(
        [w_in[:, o_c:o_qi], jnp.pad(w_in[:, o_ki:], ((0, 0), (0, LANES - (D_IDX_DIM + D_IDX_HEADS))))], axis=1)
    tm = min(TOKEN_TILE, l)
    row = pl.BlockSpec((1, tm, d), lambda i, j: (i, j, 0))
    return pl.pallas_call(
        _dsa_proj_kernel,
        grid=(b, l // tm),
        in_specs=[row, _full((1, d)), _full((d, nq)), _full((d, D_LAT + LANES)), _full((d, nqi)),
                  _full((1, D_LAT))],
        out_specs=[pl.BlockSpec((1, D_HEADS, tm, D_LAT), lambda i, j: (i, 0, j, 0)),
                   pl.BlockSpec((1, tm, D_LAT), lambda i, j: (i, j, 0)),
                   pl.BlockSpec((1, tm // DSA_SCORE_ROWS, D_LAT, DSA_SCORE_ROWS), lambda i, j: (i, j, 0, 0)),
                   pl.BlockSpec((1, D_IDX_HEADS, tm, D_IDX_DIM), lambda i, j: (i, 0, j, 0)),
                   pl.BlockSpec((1, tm, D_IDX_DIM), lambda i, j: (i, j, 0)),
                   pl.BlockSpec((1, D_IDX_HEADS, tm), lambda i, j: (i, 0, j))],
        out_shape=[jax.ShapeDtypeStruct((b, D_HEADS, l, D_LAT), bf16),
                   jax.ShapeDtypeStruct((b, l, D_LAT), bf16),
                   jax.ShapeDtypeStruct((b, l // DSA_SCORE_ROWS, D_LAT, DSA_SCORE_ROWS), bf16),
                   jax.ShapeDtypeStruct((b, D_IDX_HEADS, l, D_IDX_DIM), bf16),
                   jax.ShapeDtypeStruct((b, l, D_IDX_DIM), bf16),
                   jax.ShapeDtypeStruct((b, D_IDX_HEADS, l), f32)],
        compiler_params=_params("parallel", "parallel"),
    )(x, g.reshape(1, d), w_in[:, :o_c], w_ckw, w_in[:, o_qi:o_ki], kv_g.reshape(1, D_LAT))


def _bit_transpose(words):
    a = list(reversed(words))
    j, mask = WORD_BITS // 2, 0x0000FFFF
    while j:
        k = 0
        while k < WORD_BITS:
            t = (a[k] ^ lax.shift_right_logical(a[k + j], jnp.int32(j))) & mask
            a[k] = a[k] ^ t
            a[k + j] = a[k + j] ^ (t << j)
            k = (k + j + 1) & ~j
        j >>= 1
        mask ^= mask << j
    return a


def _dsa_attn_kernel(q_ref, qi_ref, wit_ref, c_ref, ct_ref, ki_ref, o_ref,
                     key_ref, plane_ref, qx_ref, lg_ref, m_ref, acc_ref, *, topk):
    tq = q_ref.shape[2]
    tk = DSA_K_TILE
    ts = DSA_SCORE_ROWS
    nrow = D_HEADS * tq
    n_word = plane_ref.shape[1]
    t0 = pl.program_id(1) * tq
    n_blk = (t0 + tq + ts - 1) // ts
    kf = float(topk)

    @pl.when((pl.program_id(0) == 0) & (pl.program_id(1) == 0))
    def _():
        plane_ref[...] = jnp.zeros(plane_ref.shape, i32)

    qi = qi_ref[0].reshape(D_IDX_HEADS * tq, D_IDX_DIM)
    wit = wit_ref[0]
    key_pos = lax.broadcasted_iota(i32, (ts, tq), 0)
    qry_pos = t0 + lax.broadcasted_iota(i32, (ts, tq), 1)

    def score_block(b):
        k0 = pl.multiple_of(b * ts, ts)
        lg = _dot_nt(ki_ref[0, pl.ds(k0, ts), :], qi)
        s = None
        for hd in range(D_IDX_HEADS):
            term = jnp.maximum(lg[:, hd * tq:(hd + 1) * tq], 0.0) * wit[hd:hd + 1, :]
            s = term if s is None else s + term
        s = jnp.where(s == 0.0, 0.0, s)
        bits = pltpu.bitcast(s, i32)
        key = bits ^ ((bits >> 31) & 0x7FFFFFFF)
        key = jnp.where(k0 + key_pos <= qry_pos, key, INT_MIN)
        key_ref[pl.ds(k0, ts), :] = key
        unsigned = (key ^ INT_MIN).reshape(WORD_BITS, SUBLANES, tq)
        planes = _bit_transpose([unsigned[m] for m in range(WORD_BITS)])
        w0 = pl.multiple_of(b * SUBLANES, SUBLANES)
        for p in range(WORD_BITS):
            plane_ref[p, pl.ds(w0, SUBLANES), :] = planes[p]

    def score_pair(j, carry):
        score_block(2 * j)
        score_block(2 * j + 1)
        return carry

    lax.fori_loop(0, n_blk // 2, score_pair, 0)

    @pl.when(n_blk % 2 == 1)
    def _():
        score_block(n_blk - 1)

    word = lax.broadcasted_iota(i32, (n_word, tq), 0)
    first_row = (word >> 3) * ts + (word & (SUBLANES - 1))
    n_valid = ((t0 + lax.broadcasted_iota(i32, (n_word, tq), 1) - first_row) >> 3) + 1
    cand0 = jnp.where(n_valid >= WORD_BITS, -1,
                      lax.shift_left(jnp.int32(1), jnp.clip(n_valid, 0, WORD_BITS - 1)) - 1)

    def row_sum(v):
        parts = [v[r:r + SUBLANES] for r in range(0, n_word, SUBLANES)]
        while len(parts) > 1:
            parts = [a + b for a, b in zip(parts[::2], parts[1::2])] + parts[len(parts) & ~1:]
        return jnp.sum(parts[0].astype(f32), axis=0, keepdims=True)

    def plane_body(p, carry):
        cand, need, thr = carry
        ones = cand & plane_ref[p]
        n_ones = row_sum(lax.population_count(ones))
        take = n_ones >= need
        thr = jnp.where(take, thr | lax.shift_left(jnp.int32(1), WORD_BITS - 1 - p), thr)
        return jnp.where(take, ones, cand ^ ones), jnp.where(take, need, need - n_ones), thr

    cand, keep_n, thr = lax.fori_loop(
        0, WORD_BITS, plane_body, (cand0, jnp.full((1, tq), kf, f32), jnp.zeros((1, tq), i32)))
    n_tied = row_sum(lax.population_count(cand))
    few = t0 + lax.broadcasted_iota(i32, (1, tq), 1) + 1 < topk
    thr_row = jnp.where(few, INT_MIN + 1, thr ^ INT_MIN)

    @pl.when(jnp.max(jnp.where(few, 0.0, n_tied - keep_n)) > 0.0)
    def _():
        tri = (lax.broadcasted_iota(i32, (LANES, LANES), 1)
               <= lax.broadcasted_iota(i32, (LANES, LANES), 0)).astype(bf16)

        def tie_body(j, run):
            k0 = pl.multiple_of(j * LANES, LANES)
            blk = key_ref[pl.ds(k0, LANES), :]
            tie = jnp.where(blk == thr_row, 1.0, 0.0)
            rank = run + _dot(tri, tie.astype(bf16))
            drop = (tie * jnp.where(rank > keep_n, 1.0, 0.0)) > 0.5
            key_ref[pl.ds(k0, LANES), :] = jnp.where(drop, INT_MIN, blk)
            return run + jnp.sum(tie, axis=0, keepdims=True)

        lax.fori_loop(0, n_blk * (ts // LANES), tie_body, jnp.zeros((1, tq), f32))

    n_cb = nrow // ATT_COLS
    heads_per_cb = ATT_COLS // tq
    m_ref[...] = jnp.full(m_ref.shape, M_INIT, f32)
    acc_ref[...] = jnp.zeros(acc_ref.shape, f32)
    onehot = (lax.broadcasted_iota(i32, (tq, ATT_COLS), 0)
              == (lax.broadcasted_iota(i32, (tq, ATT_COLS), 1) & (tq - 1)))
    onehot = jnp.where(onehot, 1.0, 0.0).astype(bf16)
    for cb in range(n_cb):
        for i in range(heads_per_cb):
            qx_ref[cb, :D_LAT, i * tq:(i + 1) * tq] = q_ref[0, cb * heads_per_cb + i].astype(f32).T.astype(bf16)
        qx_ref[cb, D_LAT:, :] = onehot
    scale = D_LAT ** -0.5 * LOG2E

    def col_max(v):
        parts = [v[r:r + SUBLANES] for r in range(0, v.shape[0], SUBLANES)]
        while len(parts) > 1:
            parts = [jnp.maximum(a, b) for a, b in zip(parts[::2], parts[1::2])] + parts[len(parts) & ~1:]
        return jnp.max(parts[0], axis=0, keepdims=True)

    def att_tile(blk0, n_sub):
        width = n_sub * ts
        k0 = pl.multiple_of(blk0 * ts, ts)
        mask_t = jnp.where(key_ref[pl.ds(k0, width), :] >= thr_row, 0.0, NEG_BIAS).astype(bf16)
        keys = jnp.concatenate([c_ref[0, pl.ds(k0, width), :], mask_t], axis=1)
        vals_t = jnp.concatenate([ct_ref[0, blk0 + i] for i in range(n_sub)], axis=1)
        vals_t = jnp.concatenate([vals_t, jnp.ones((SUM_ROWS, width), bf16)], axis=0)
        for cb in range(min(QK_AHEAD, n_cb)):
            lg_ref[cb, :width, :] = _dot(keys, qx_ref[cb])
        for cb in range(n_cb):
            if cb + QK_AHEAD < n_cb:
                lg_ref[cb + QK_AHEAD, :width, :] = _dot(keys, qx_ref[cb + QK_AHEAD])
            m_prev = m_ref[cb]
            m_new = m_prev
            for r0 in range(0, width, SOFTMAX_ROWS):
                m_new = jnp.maximum(m_new, col_max(lg_ref[cb, r0:r0 + SOFTMAX_ROWS, :]))
            alpha = jnp.exp2((m_prev - m_new) * scale)
            p = jnp.concatenate(
                [jnp.exp2((lg_ref[cb, r0:r0 + SOFTMAX_ROWS, :] - m_new) * scale).astype(bf16)
                 for r0 in range(0, width, SOFTMAX_ROWS)], axis=0)
            m_ref[cb] = m_new
            acc_ref[cb] = acc_ref[cb] * alpha + _dot(vals_t, p)

    def att_body(j, carry):
        att_tile(j * (tk // ts), tk // ts)
        return carry

    lax.fori_loop(0, n_blk // (tk // ts), att_body, 0)

    @pl.when(n_blk % (tk // ts) == 1)
    def _():
        att_tile(n_blk - 1, 1)

    for cb in range(n_cb):
        out_t = acc_ref[cb, :D_LAT, :] / acc_ref[cb, D_LAT:D_LAT + 1, :]
        for i in range(heads_per_cb):
            hd = cb * heads_per_cb + i
            o_ref[0, :, hd * D_LAT:(hd + 1) * D_LAT] = out_t[:, i * tq:(i + 1) * tq].T.astype(bf16)


def _dsa_attn(q, qi, wit, c, ct, ki, topk):
    b, _, l, _ = q.shape
    tq = DSA_Q_TILE
    assert tq == LANES and l % DSA_K_TILE == 0 and DSA_K_TILE == 2 * DSA_SCORE_ROWS
    assert DSA_SCORE_ROWS == WORD_BITS * SUBLANES
    n_cb = D_HEADS * tq // ATT_COLS
    return pl.pallas_call(
        functools.partial(_dsa_attn_kernel, topk=topk),
        grid=(b, l // tq),
        in_specs=[pl.BlockSpec((1, D_HEADS, tq, D_LAT), lambda i, j: (i, 0, j, 0)),
                  pl.BlockSpec((1, D_IDX_HEADS, tq, D_IDX_DIM), lambda i, j: (i, 0, j, 0)),
                  pl.BlockSpec((1, D_IDX_HEADS, tq), lambda i, j: (i, 0, j)),
                  pl.BlockSpec((1, l, D_LAT), lambda i, j: (i, 0, 0)),
                  pl.BlockSpec((1, l // DSA_SCORE_ROWS, D_LAT, DSA_SCORE_ROWS), lambda i, j: (i, 0, 0, 0)),
                  pl.BlockSpec((1, l, D_IDX_DIM), lambda i, j: (i, 0, 0))],
        out_specs=pl.BlockSpec((1, tq, D_HEADS * D_LAT), lambda i, j: (i, j, 0)),
        out_shape=jax.ShapeDtypeStruct((b, l, D_HEADS * D_LAT), bf16),
        scratch_shapes=[pltpu.VMEM((l, tq), i32),
                        pltpu.VMEM((WORD_BITS, l // WORD_BITS, tq), i32),
                        pltpu.VMEM((n_cb, D_LAT + tq, ATT_COLS), bf16),
                        pltpu.VMEM((n_cb, DSA_K_TILE, ATT_COLS), f32),
                        pltpu.VMEM((n_cb, 1, ATT_COLS), f32),
                        pltpu.VMEM((n_cb, D_LAT + SUM_ROWS, ATT_COLS), f32)],
        compiler_params=_params("arbitrary", "arbitrary"),
    )(q, qi, wit, c, ct, ki)


def _dsa_out_kernel(x_ref, o_ref, wuv_ref, wout_ref, y_ref, ov_ref):
    for hd in range(D_HEADS):
        ov_ref[:, hd * D_VDIM:(hd + 1) * D_VDIM] = _dot(o_ref[:, hd * D_LAT:(hd + 1) * D_LAT], wuv_ref[hd])
    y_ref[...] = x_ref[...] + _dot(ov_ref[...].astype(bf16), wout_ref[...])


def _dsa_out(x, o, w_uv, w_out):
    n, d = x.shape
    tm = min(TOKEN_TILE, n)
    return pl.pallas_call(
        _dsa_out_kernel,
        grid=(n // tm,),
        in_specs=[pl.BlockSpec((tm, d), lambda i: (i, 0)),
                  pl.BlockSpec((tm, D_HEADS * D_LAT), lambda i: (i, 0)),
                  _full(w_uv.shape), _full(w_out.shape)],
        out_specs=pl.BlockSpec((tm, d), lambda i: (i, 0)),
        out_shape=jax.ShapeDtypeStruct((n, d), f32),
        scratch_shapes=[pltpu.VMEM((tm, D_HEADS * D_VDIM), f32)],
        compiler_params=_params("parallel"),
    )(x, o, w_uv.astype(bf16), w_out.astype(bf16))


def _mixer_dsa(x, g, w_in, kv_g, w_uv, w_out):
    b, l, d = x.shape
    topk = min(D_TOPK_MAX, l // 4)
    q, c, ct, qi, ki, wit = _dsa_proj(x, g, w_in, kv_g)
    o = _dsa_attn(q, qi, wit, c, ct, ki, topk)
    return _dsa_out(x.reshape(b * l, d), o.reshape(b * l, D_HEADS * D_LAT), w_uv, w_out).reshape(b, l, d)


@jax.jit
def _trunk(x, norm_mix_g, norm_mlp_g, final_g, a_w_in, a_v_g, a_w_s, a_b_s, a_w_out, b_w_in, b_conv_w,
           b_w_out, c_w_in, c_w_grp, c_scale, d_w_in, d_kv_g, d_w_uv, d_w_out, mlp_w1, mlp_w2):
    b, l, d = x.shape
    depth = norm_mix_g.shape[0]
    n_mixers = 4
    for i in range(depth):
        m, j = i % n_mixers, i // n_mixers
        g = norm_mix_g[i]
        if m == 0:
            x = _mixer_sgu(x.reshape(b * l, d), g, a_w_in[j], a_v_g[j], a_w_s[j], a_b_s[j],
                           a_w_out[j]).reshape(b, l, d)
        elif m == 1:
            x = _mixer_conv(x, g, b_w_in[j], b_conv_w[j], b_w_out[j])
        elif m == 2:
            x = _mixer_pool(x, g, c_w_in[j], c_w_grp[j], c_scale[j])
        else:
            x = _mixer_dsa(x, g, d_w_in[j], d_kv_g[j], d_w_uv[j], d_w_out[j])
        x = _mlp(x.reshape(b * l, d), norm_mlp_g[i], mlp_w1[i], mlp_w2[i], final_g,
                 apply_final=(i == depth - 1)).reshape(b, l, d)
    return x


def kernel(x, norm_mix_g, norm_mlp_g, final_g, a_w_in, a_v_g, a_w_s, a_b_s, a_w_out, b_w_in, b_conv_w,
           b_w_out, c_w_in, c_w_grp, c_scale, d_w_in, d_kv_g, d_w_uv, d_w_out, mlp_w1, mlp_w2):
    return _trunk(x, norm_mix_g, norm_mlp_g, final_g, a_w_in, a_v_g, a_w_s, a_b_s, a_w_out, b_w_in,
                  b_conv_w, b_w_out, c_w_in, c_w_grp, c_scale, d_w_in, d_kv_g, d_w_uv, d_w_out,
                  mlp_w1, mlp_w2)
```

```python
import functools

import jax
import jax.numpy as jnp
from jax import lax
from jax.experimental import pallas as pl
from jax.experimental.pallas import tpu as pltpu

EPS = 1e-6
LANES = 128
VMEM_LIMIT_BYTES = 56 * 1024 * 1024

A_CHUNK = 128
A_GROUPS = 8
B_KSIZE = 3
C_WINDOWS = (2, 4, 8, 16)
D_HEADS = 16
D_LAT = 128
D_VDIM = 64
D_IDX_HEADS = 8
D_IDX_DIM = 64
D_TOPK_MAX = 256

INT_MIN = -(2 ** 31)
NEG_BIAS = -1e30
M_INIT = -1e29
LOG2E = 1.4426950408889634

TOKEN_TILE = 1024
DSA_Q_TILE = 128
DSA_K_TILE = 512
DSA_SCORE_ROWS = 256
ATT_COLS = 256
QK_AHEAD = 5
SOFTMAX_ROWS = 128
SUM_ROWS = 16
SUBLANES = 8
WORD_BITS = 32

f32 = jnp.float32
bf16 = jnp.bfloat16
i32 = jnp.int32


def _dot(a, b):
    return jnp.dot(a, b, preferred_element_type=f32)


def _dot_nt(a, b):
    return lax.dot_general(a, b, (((1,), (1,)), ((), ())), preferred_element_type=f32)


def _rms(x, g):
    ms = jnp.mean(x * x, axis=-1, keepdims=True)
    return x * lax.rsqrt(ms + EPS) * g


def _params(*sem):
    return pltpu.CompilerParams(dimension_semantics=sem, vmem_limit_bytes=VMEM_LIMIT_BYTES)


def _full(shape):
    nd = len(shape)
    return pl.BlockSpec(shape, lambda *_: (0,) * nd)


def _mlp_kernel(x_ref, g_ref, w1_ref, w2_ref, fg_ref, o_ref, *, ff_chunk, apply_final):
    x = x_ref[...]
    h = _rms(x, g_ref[...]).astype(bf16)
    acc = x
    for c in range(w1_ref.shape[1] // ff_chunk):
        a = jnp.maximum(_dot(h, w1_ref[:, c * ff_chunk:(c + 1) * ff_chunk]), 0.0)
        acc = acc + _dot((a * a).astype(bf16), w2_ref[c * ff_chunk:(c + 1) * ff_chunk, :])
    if apply_final:
        acc = _rms(acc, fg_ref[...])
    o_ref[...] = acc


def _mlp(x, g, w1, w2, final_g, apply_final):
    n, d = x.shape
    dff = w1.shape[1]
    tm = min(TOKEN_TILE, n)
    row = pl.BlockSpec((tm, d), lambda i: (i, 0))
    return pl.pallas_call(
        functools.partial(_mlp_kernel, ff_chunk=min(1024, dff), apply_final=apply_final),
        grid=(n // tm,),
        in_specs=[row, _full((1, d)), _full((d, dff)), _full((dff, d)), _full((1, d))],
        out_specs=row,
        out_shape=jax.ShapeDtypeStruct((n, d), f32),
        compiler_params=_params("parallel"),
    )(x, g.reshape(1, d), w1.astype(bf16), w2.astype(bf16), final_g.reshape(1, d))


def _sgu_kernel(x_ref, g_ref, win_ref, vg_ref, ws_ref, bst_ref, wout_ref, o_ref, s_ref):
    tm, _ = x_ref.shape
    width = wout_ref.shape[0]
    gdim = width // A_GROUPS
    x = x_ref[...]
    h = _rms(x, g_ref[...]).astype(bf16)
    u = jax.nn.gelu(_dot(h, win_ref[:, :width]))
    v = jax.nn.gelu(_dot(h, win_ref[:, width:]))
    v = _rms(v, vg_ref[...]).astype(bf16)
    r = lax.broadcasted_iota(i32, (A_CHUNK, A_CHUNK), 0)
    c = lax.broadcasted_iota(i32, (A_CHUNK, A_CHUNK), 1)
    causal = c <= r
    bst = bst_ref[...]
    for gi in range(A_GROUPS):
        ws = jnp.where(causal, ws_ref[gi], 0.0).astype(bf16)
        bias = jnp.broadcast_to(bst[:, gi:gi + 1], (A_CHUNK, gdim))
        for n in range(tm // A_CHUNK):
            vb = v[n * A_CHUNK:(n + 1) * A_CHUNK, gi * gdim:(gi + 1) * gdim]
            s_ref[n * A_CHUNK:(n + 1) * A_CHUNK, gi * gdim:(gi + 1) * gdim] = _dot(ws, vb) + bias
    o_ref[...] = x + _dot((u * s_ref[...]).astype(bf16), wout_ref[...])


def _mixer_sgu(x, g, w_in, v_g, w_s, b_s, w_out):
    n, d = x.shape
    width = w_out.shape[0]
    tm = min(TOKEN_TILE, n)
    row = pl.BlockSpec((tm, d), lambda i: (i, 0))
    return pl.pallas_call(
        _sgu_kernel,
        grid=(n // tm,),
        in_specs=[row, _full((1, d)), _full((d, 2 * width)), _full((1, width)),
                  _full(w_s.shape), _full((A_CHUNK, A_GROUPS)), _full((width, d))],
        out_specs=row,
        out_shape=jax.ShapeDtypeStruct((n, d), f32),
        scratch_shapes=[pltpu.VMEM((tm, width), f32)],
        compiler_params=_params("parallel"),
    )(x, g.reshape(1, d), w_in.astype(bf16), v_g.reshape(1, width), w_s, b_s.T, w_out.astype(bf16))


CONV_HALO = 8


def _conv_kernel(x_ref, g_ref, win_ref, cw_ref, wout_ref, o_ref, z_ref):
    tm = x_ref.shape[1]
    width = wout_ref.shape[0]

    @pl.when(pl.program_id(1) == 0)
    def _():
        z_ref[0:CONV_HALO, :] = jnp.zeros((CONV_HALO, width), f32)

    x = x_ref[0]
    h = _rms(x, g_ref[...]).astype(bf16)
    bg = _dot(h, win_ref[:, :width])
    z = _dot(h, win_ref[:, width:2 * width]) * _dot(h, win_ref[:, 2 * width:])
    z_ref[CONV_HALO:CONV_HALO + tm, :] = z
    y = cw_ref[B_KSIZE - 1:B_KSIZE, :] * z
    for k in range(B_KSIZE - 1):
        shift = B_KSIZE - 1 - k
        y = y + cw_ref[k:k + 1, :] * z_ref[CONV_HALO - shift:CONV_HALO - shift + tm, :]
    z_ref[0:CONV_HALO, :] = z[tm - CONV_HALO:, :]
    o_ref[0] = x + _dot((bg * y).astype(bf16), wout_ref[...])


def _mixer_conv(x, g, w_in, conv_w, w_out):
    b, l, d = x.shape
    width = w_out.shape[0]
    tm = min(TOKEN_TILE, l)
    row = pl.BlockSpec((1, tm, d), lambda i, j: (i, j, 0))
    return pl.pallas_call(
        _conv_kernel,
        grid=(b, l // tm),
        in_specs=[row, _full((1, d)), _full((d, 3 * width)), _full((B_KSIZE, width)), _full((width, d))],
        out_specs=row,
        out_shape=jax.ShapeDtypeStruct((b, l, d), f32),
        scratch_shapes=[pltpu.VMEM((CONV_HALO + tm, width), f32)],
        compiler_params=_params("arbitrary", "arbitrary"),
    )(x, g.reshape(1, d), w_in.astype(bf16), conv_w.reshape(B_KSIZE, width), w_out.astype(bf16))


POOL_HALO = 16
POOL_PAD = 8


def _pool_kernel(x_ref, g_ref, win_ref, wgrp_ref, sc_ref, o_ref, carry_ref, pa_ref, pb_ref, y_ref):
    tm = x_ref.shape[1]
    d = x_ref.shape[2]
    gdim = d // len(C_WINDOWS)
    n_ext = tm + POOL_HALO
    j = pl.program_id(1)

    @pl.when(j == 0)
    def _():
        carry_ref[...] = jnp.zeros(carry_ref.shape, f32)

    x = x_ref[0]
    h = _rms(x, g_ref[...]).astype(bf16)
    z = _dot(h, win_ref[...])
    pa_ref[0:POOL_PAD, :] = jnp.zeros((POOL_PAD, gdim), f32)
    pb_ref[0:POOL_PAD, :] = jnp.zeros((POOL_PAD, gdim), f32)
    pos = j * tm + lax.broadcasted_iota(i32, (tm, 1), 0) + 1
    for gi, w in enumerate(C_WINDOWS):
        zg = z[:, gi * gdim:(gi + 1) * gdim]
        src, dst = pa_ref, pb_ref
        src[POOL_PAD:POOL_PAD + POOL_HALO, :] = carry_ref[:, gi * gdim:(gi + 1) * gdim]
        src[POOL_PAD + POOL_HALO:, :] = zg
        shift = 1
        while True:
            acc = src[POOL_PAD:, :] + src[POOL_PAD - shift:POOL_PAD - shift + n_ext, :]
            shift *= 2
            if shift == w:
                break
            dst[POOL_PAD:, :] = acc
            src, dst = dst, src
        cnt = jnp.minimum(pos, w).astype(f32)
        pooled = acc[POOL_HALO:, :] / cnt - zg
        y_ref[:, gi * gdim:(gi + 1) * gdim] = _dot(pooled.astype(bf16), wgrp_ref[gi])
    carry_ref[...] = z[tm - POOL_HALO:, :]
    o_ref[0] = x + y_ref[...] * sc_ref[...]


def _mixer_pool(x, g, w_in, w_grp, scale):
    b, l, d = x.shape
    gdim = d // len(C_WINDOWS)
    tm = min(TOKEN_TILE, l)
    row = pl.BlockSpec((1, tm, d), lambda i, j: (i, j, 0))
    ext = POOL_PAD + POOL_HALO + tm
    return pl.pallas_call(
        _pool_kernel,
        grid=(b, l // tm),
        in_specs=[row, _full((1, d)), _full((d, d)), _full(w_grp.shape), _full((1, d))],
        out_specs=row,
        out_shape=jax.ShapeDtypeStruct((b, l, d), f32),
        scratch_shapes=[pltpu.VMEM((POOL_HALO, d), f32), pltpu.VMEM((ext, gdim), f32),
                        pltpu.VMEM((ext, gdim), f32), pltpu.VMEM((tm, d), f32)],
        compiler_params=_params("arbitrary", "arbitrary"),
    )(x, g.reshape(1, d), w_in.astype(bf16), w_grp.astype(bf16), scale.reshape(1, d))


def _dsa_proj_kernel(x_ref, g_ref, wq_ref, wckw_ref, wqi_ref, kvg_ref,
                     q_ref, c_ref, ct_ref, qi_ref, ki_ref, wit_ref):
    tm = x_ref.shape[1]
    h = _rms(x_ref[0], g_ref[...]).astype(bf16)
    q = _dot(h, wq_ref[...])
    for hd in range(D_HEADS):
        q_ref[0, hd] = q[:, hd * D_LAT:(hd + 1) * D_LAT].astype(bf16)
    ckw = _dot(h, wckw_ref[...])
    c = _rms(ckw[:, :D_LAT], kvg_ref[...])
    kw = ckw[:, D_LAT:]
    c_ref[0] = c.astype(bf16)
    for blk in range(tm // DSA_SCORE_ROWS):
        rows = slice(blk * DSA_SCORE_ROWS, (blk + 1) * DSA_SCORE_ROWS)
        ct_ref[0, blk] = c[rows, :].T.astype(bf16)
    qi = _dot(h, wqi_ref[...])
    for hd in range(D_IDX_HEADS):
        qi_ref[0, hd] = qi[:, hd * D_IDX_DIM:(hd + 1) * D_IDX_DIM].astype(bf16)
    ki_ref[0] = kw[:, :D_IDX_DIM].astype(bf16)
    wit_ref[0] = kw.T[D_IDX_DIM:D_IDX_DIM + D_IDX_HEADS, :] * (D_IDX_HEADS ** -0.5 * D_IDX_DIM ** -0.5)


def _dsa_proj(x, g, w_in, kv_g):
    b, l, d = x.shape
    nq = D_HEADS * D_LAT
    nqi = D_IDX_HEADS * D_IDX_DIM
    o_c, o_qi, o_ki = nq, nq + D_LAT, nq + D_LAT + nqi
    w_in = w_in.astype(bf16)
    w_ckw = jnp.concatenate(
        [w_in[:, o_c:o_qi], jnp.pad(w_in[:, o_ki:], ((0, 0), (0, LANES - (D_IDX_DIM + D_IDX_HEADS))))], axis=1)
    tm = min(TOKEN_TILE, l)
    row = pl.BlockSpec((1, tm, d), lambda i, j: (i, j, 0))
    return pl.pallas_call(
        _dsa_proj_kernel,
        grid=(b, l // tm),
        in_specs=[row, _full((1, d)), _full((d, nq)), _full((d, D_LAT + LANES)), _full((d, nqi)),
                  _full((1, D_LAT))],
        out_specs=[pl.BlockSpec((1, D_HEADS, tm, D_LAT), lambda i, j: (i, 0, j, 0)),
                   pl.BlockSpec((1, tm, D_LAT), lambda i, j: (i, j, 0)),
                   pl.BlockSpec((1, tm // DSA_SCORE_ROWS, D_LAT, DSA_SCORE_ROWS), lambda i, j: (i, j, 0, 0)),
                   pl.BlockSpec((1, D_IDX_HEADS, tm, D_IDX_DIM), lambda i, j: (i, 0, j, 0)),
                   pl.BlockSpec((1, tm, D_IDX_DIM), lambda i, j: (i, j, 0)),
                   pl.BlockSpec((1, D_IDX_HEADS, tm), lambda i, j: (i, 0, j))],
        out_shape=[jax.ShapeDtypeStruct((b, D_HEADS, l, D_LAT), bf16),
                   jax.ShapeDtypeStruct((b, l, D_LAT), bf16),
                   jax.ShapeDtypeStruct((b, l // DSA_SCORE_ROWS, D_LAT, DSA_SCORE_ROWS), bf16),
                   jax.ShapeDtypeStruct((b, D_IDX_HEADS, l, D_IDX_DIM), bf16),
                   jax.ShapeDtypeStruct((b, l, D_IDX_DIM), bf16),
                   jax.ShapeDtypeStruct((b, D_IDX_HEADS, l), f32)],
        compiler_params=_params("parallel", "parallel"),
    )(x, g.reshape(1, d), w_in[:, :o_c], w_ckw, w_in[:, o_qi:o_ki], kv_g.reshape(1, D_LAT))


def _bit_transpose(words):
    a = list(reversed(words))
    j, mask = WORD_BITS // 2, 0x0000FFFF
    while j:
        k = 0
        while k < WORD_BITS:
            t = (a[k] ^ lax.shift_right_logical(a[k + j], jnp.int32(j))) & mask
            a[k] = a[k] ^ t
            a[k + j] = a[k + j] ^ (t << j)
            k = (k + j + 1) & ~j
        j >>= 1
        mask ^= mask << j
    return a


def _dsa_attn_kernel(q_ref, qi_ref, wit_ref, c_ref, ct_ref, ki_ref, o_ref,
                     key_ref, plane_ref, qx_ref, lg_ref, m_ref, acc_ref, *, topk):
    tq = q_ref.shape[2]
    tk = DSA_K_TILE
    ts = DSA_SCORE_ROWS
    nrow = D_HEADS * tq
    n_word = plane_ref.shape[1]
    t0 = pl.program_id(1) * tq
    n_blk = (t0 + tq + ts - 1) // ts
    kf = float(topk)

    @pl.when((pl.program_id(0) == 0) & (pl.program_id(1) == 0))
    def _():
        plane_ref[...] = jnp.zeros(plane_ref.shape, i32)

    qi = qi_ref[0].reshape(D_IDX_HEADS * tq, D_IDX_DIM)
    wit = wit_ref[0]
    key_pos = lax.broadcasted_iota(i32, (ts, tq), 0)
    qry_pos = t0 + lax.broadcasted_iota(i32, (ts, tq), 1)

    def score_block(b):
        k0 = pl.multiple_of(b * ts, ts)
        lg = _dot_nt(ki_ref[0, pl.ds(k0, ts), :], qi)
        s = None
        for hd in range(D_IDX_HEADS):
            term = jnp.maximum(lg[:, hd * tq:(hd + 1) * tq], 0.0) * wit[hd:hd + 1, :]
            s = term if s is None else s + term
        s = jnp.where(s == 0.0, 0.0, s)
        bits = pltpu.bitcast(s, i32)
        key = bits ^ ((bits >> 31) & 0x7FFFFFFF)
        key = jnp.where(k0 + key_pos <= qry_pos, key, INT_MIN)
        key_ref[pl.ds(k0, ts), :] = key
        unsigned = (key ^ INT_MIN).reshape(WORD_BITS, SUBLANES, tq)
        planes = _bit_transpose([unsigned[m] for m in range(WORD_BITS)])
        w0 = pl.multiple_of(b * SUBLANES, SUBLANES)
        for p in range(WORD_BITS):
            plane_ref[p, pl.ds(w0, SUBLANES), :] = planes[p]

    def score_pair(j, carry):
        score_block(2 * j)
        score_block(2 * j + 1)
        return carry

    lax.fori_loop(0, n_blk // 2, score_pair, 0)

    @pl.when(n_blk % 2 == 1)
    def _():
        score_block(n_blk - 1)

    word = lax.broadcasted_iota(i32, (n_word, tq), 0)
    first_row = (word >> 3) * ts + (word & (SUBLANES - 1))
    n_valid = ((t0 + lax.broadcasted_iota(i32, (n_word, tq), 1) - first_row) >> 3) + 1
    cand0 = jnp.where(n_valid >= WORD_BITS, -1,
                      lax.shift_left(jnp.int32(1), jnp.clip(n_valid, 0, WORD_BITS - 1)) - 1)

    def row_sum(v):
        parts = [v[r:r + SUBLANES] for r in range(0, n_word, SUBLANES)]
        while len(parts) > 1:
            parts = [a + b for a, b in zip(parts[::2], parts[1::2])] + parts[len(parts) & ~1:]
        return jnp.sum(parts[0].astype(f32), axis=0, keepdims=True)

    def plane_body(p, carry):
        cand, need, thr = carry
        ones = cand & plane_ref[p]
        n_ones = row_sum(lax.population_count(ones))
        take = n_ones >= need
        thr = jnp.where(take, thr | lax.shift_left(jnp.int32(1), WORD_BITS - 1 - p), thr)
        return jnp.where(take, ones, cand ^ ones), jnp.where(take, need, need - n_ones), thr

    cand, keep_n, thr = lax.fori_loop(
        0, WORD_BITS, plane_body, (cand0, jnp.full((1, tq), kf, f32), jnp.zeros((1, tq), i32)))
    n_tied = row_sum(lax.population_count(cand))
    few = t0 + lax.broadcasted_iota(i32, (1, tq), 1) + 1 < topk
    thr_row = jnp.where(few, INT_MIN + 1, thr ^ INT_MIN)

    @pl.when(jnp.max(jnp.where(few, 0.0, n_tied - keep_n)) > 0.0)
    def _():
        tri = (lax.broadcasted_iota(i32, (LANES, LANES), 1)
               <= lax.broadcasted_iota(i32, (LANES, LANES), 0)).astype(bf16)

        def tie_body(j, run):
            k0 = pl.multiple_of(j * LANES, LANES)
            blk = key_ref[pl.ds(k0, LANES), :]
            tie = jnp.where(blk == thr_row, 1.0, 0.0)
            rank = run + _dot(tri, tie.astype(bf16))
            drop = (tie * jnp.where(rank > keep_n, 1.0, 0.0)) > 0.5
            key_ref[pl.ds(k0, LANES), :] = jnp.where(drop, INT_MIN, blk)
            return run + jnp.sum(tie, axis=0, keepdims=True)

        lax.fori_loop(0, n_blk * (ts // LANES), tie_body, jnp.zeros((1, tq), f32))

    n_cb = nrow // ATT_COLS
    heads_per_cb = ATT_COLS // tq
    m_ref[...] = jnp.full(m_ref.shape, M_INIT, f32)
    acc_ref[...] = jnp.zeros(acc_ref.shape, f32)
    onehot = (lax.broadcasted_iota(i32, (tq, ATT_COLS), 0)
              == (lax.broadcasted_iota(i32, (tq, ATT_COLS), 1) & (tq - 1)))
    onehot = jnp.where(onehot, 1.0, 0.0).astype(bf16)
    for cb in range(n_cb):
        for i in range(heads_per_cb):
            qx_ref[cb, :D_LAT, i * tq:(i + 1) * tq] = q_ref[0, cb * heads_per_cb + i].astype(f32).T.astype(bf16)
        qx_ref[cb, D_LAT:, :] = onehot
    scale = D_LAT ** -0.5 * LOG2E

    def col_max(v):
        parts = [v[r:r + SUBLANES] for r in range(0, v.shape[0], SUBLANES)]
        while len(parts) > 1:
            parts = [jnp.maximum(a, b) for a, b in zip(parts[::2], parts[1::2])] + parts[len(parts) & ~1:]
        return jnp.max(parts[0], axis=0, keepdims=True)

    def att_tile(blk0, n_sub):
        width = n_sub * ts
        k0 = pl.multiple_of(blk0 * ts, ts)
        mask_t = jnp.where(key_ref[pl.ds(k0, width), :] >= thr_row, 0.0, NEG_BIAS).astype(bf16)
        keys = jnp.concatenate([c_ref[0, pl.ds(k0, width), :], mask_t], axis=1)
        vals_t = jnp.concatenate([ct_ref[0, blk0 + i] for i in range(n_sub)], axis=1)
        vals_t = jnp.concatenate([vals_t, jnp.ones((SUM_ROWS, width), bf16)], axis=0)
        for cb in range(min(QK_AHEAD, n_cb)):
            lg_ref[cb, :width, :] = _dot(keys, qx_ref[cb])
        for cb in range(n_cb):
            if cb + QK_AHEAD < n_cb:
                lg_ref[cb + QK_AHEAD, :width, :] = _dot(keys, qx_ref[cb + QK_AHEAD])
            m_prev = m_ref[cb]
            m_new = m_prev
            for r0 in range(0, width, SOFTMAX_ROWS):
                m_new = jnp.maximum(m_new, col_max(lg_ref[cb, r0:r0 + SOFTMAX_ROWS, :]))
            alpha = jnp.exp2((m_prev - m_new) * scale)
            p = jnp.concatenate(
                [jnp.exp2((lg_ref[cb, r0:r0 + SOFTMAX_ROWS, :] - m_new) * scale).astype(bf16)
                 for r0 in range(0, width, SOFTMAX_ROWS)], axis=0)
            m_ref[cb] = m_new
            acc_ref[cb] = acc_ref[cb] * alpha + _dot(vals_t, p)

    def att_body(j, carry):
        att_tile(j * (tk // ts), tk // ts)
        return carry

    lax.fori_loop(0, n_blk // (tk // ts), att_body, 0)

    @pl.when(n_blk % (tk // ts) == 1)
    def _():
        att_tile(n_blk - 1, 1)

    for cb in range(n_cb):
        out_t = acc_ref[cb, :D_LAT, :] / acc_ref[cb, D_LAT:D_LAT + 1, :]
        for i in range(heads_per_cb):
            hd = cb * heads_per_cb + i
            o_ref[0, :, hd * D_LAT:(hd + 1) * D_LAT] = out_t[:, i * tq:(i + 1) * tq].T.astype(bf16)


def _dsa_attn(q, qi, wit, c, ct, ki, topk):
    b, _, l, _ = q.shape
    tq = DSA_Q_TILE
    assert tq == LANES and l % DSA_K_TILE == 0 and DSA_K_TILE == 2 * DSA_SCORE_ROWS
    assert DSA_SCORE_ROWS == WORD_BITS * SUBLANES
    n_cb = D_HEADS * tq // ATT_COLS
    return pl.pallas_call(
        functools.partial(_dsa_attn_kernel, topk=topk),
        grid=(b, l // tq),
        in_specs=[pl.BlockSpec((1, D_HEADS, tq, D_LAT), lambda i, j: (i, 0, j, 0)),
                  pl.BlockSpec((1, D_IDX_HEADS, tq, D_IDX_DIM), lambda i, j: (i, 0, j, 0)),
                  pl.BlockSpec((1, D_IDX_HEADS, tq), lambda i, j: (i, 0, j)),
                  pl.BlockSpec((1, l, D_LAT), lambda i, j: (i, 0, 0)),
                  pl.BlockSpec((1, l // DSA_SCORE_ROWS, D_LAT, DSA_SCORE_ROWS), lambda i, j: (i, 0, 0, 0)),
                  pl.BlockSpec((1, l, D_IDX_DIM), lambda i, j: (i, 0, 0))],
        out_specs=pl.BlockSpec((1, tq, D_HEADS * D_LAT), lambda i, j: (i, j, 0)),
        out_shape=jax.ShapeDtypeStruct((b, l, D_HEADS * D_LAT), bf16),
        scratch_shapes=[pltpu.VMEM((l, tq), i32),
                        pltpu.VMEM((WORD_BITS, l // WORD_BITS, tq), i32),
                        pltpu.VMEM((n_cb, D_LAT + tq, ATT_COLS), bf16),
                        pltpu.VMEM((n_cb, DSA_K_TILE, ATT_COLS), f32),
                        pltpu.VMEM((n_cb, 1, ATT_COLS), f32),
                        pltpu.VMEM((n_cb, D_LAT + SUM_ROWS, ATT_COLS), f32)],
        compiler_params=_params("arbitrary", "arbitrary"),
    )(q, qi, wit, c, ct, ki)


def _dsa_out_kernel(x_ref, o_ref, wuv_ref, wout_ref, y_ref, ov_ref):
    for hd in range(D_HEADS):
        ov_ref[:, hd * D_VDIM:(hd + 1) * D_VDIM] = _dot(o_ref[:, hd * D_LAT:(hd + 1) * D_LAT], wuv_ref[hd])
    y_ref[...] = x_ref[...] + _dot(ov_ref[...].astype(bf16), wout_ref[...])


def _dsa_out(x, o, w_uv, w_out):
    n, d = x.shape
    tm = min(TOKEN_TILE, n)
    return pl.pallas_call(
        _dsa_out_kernel,
        grid=(n // tm,),
        in_specs=[pl.BlockSpec((tm, d), lambda i: (i, 0)),
                  pl.BlockSpec((tm, D_HEADS * D_LAT), lambda i: (i, 0)),
                  _full(w_uv.shape), _full(w_out.shape)],
        out_specs=pl.BlockSpec((tm, d), lambda i: (i, 0)),
        out_shape=jax.ShapeDtypeStruct((n, d), f32),
        scratch_shapes=[pltpu.VMEM((tm, D_HEADS * D_VDIM), f32)],
        compiler_params=_params("parallel"),
    )(x, o, w_uv.astype(bf16), w_out.astype(bf16))


def _mixer_dsa(x, g, w_in, kv_g, w_uv, w_out):
    b, l, d = x.shape
    topk = min(D_TOPK_MAX, l // 4)
    q, c, ct, qi, ki, wit = _dsa_proj(x, g, w_in, kv_g)
    o = _dsa_attn(q, qi, wit, c, ct, ki, topk)
    return _dsa_out(x.reshape(b * l, d), o.reshape(b * l, D_HEADS * D_LAT), w_uv, w_out).reshape(b, l, d)


@jax.jit
def _trunk(x, norm_mix_g, norm_mlp_g, final_g, a_w_in, a_v_g, a_w_s, a_b_s, a_w_out, b_w_in, b_conv_w,
           b_w_out, c_w_in, c_w_grp, c_scale, d_w_in, d_kv_g, d_w_uv, d_w_out, mlp_w1, mlp_w2):
    b, l, d = x.shape
    depth = norm_mix_g.shape[0]
    n_mixers = 4
    for i in range(depth):
        m, j = i % n_mixers, i // n_mixers
        g = norm_mix_g[i]
        if m == 0:
            x = _mixer_sgu(x.reshape(b * l, d), g, a_w_in[j], a_v_g[j], a_w_s[j], a_b_s[j],
                           a_w_out[j]).reshape(b, l, d)
        elif m == 1:
            x = _mixer_conv(x, g, b_w_in[j], b_conv_w[j], b_w_out[j])
        elif m == 2:
            x = _mixer_pool(x, g, c_w_in[j], c_w_grp[j], c_scale[j])
        else:
            x = _mixer_dsa(x, g, d_w_in[j], d_kv_g[j], d_w_uv[j], d_w_out[j])
        x = _mlp(x.reshape(b * l, d), norm_mlp_g[i], mlp_w1[i], mlp_w2[i], final_g,
                 apply_final=(i == depth - 1)).reshape(b, l, d)
    return x


def kernel(x, norm_mix_g, norm_mlp_g, final_g, a_w_in, a_v_g, a_w_s, a_b_s, a_w_out, b_w_in, b_conv_w,
           b_w_out, c_w_in, c_w_grp, c_scale, d_w_in, d_kv_g, d_w_uv, d_w_out, mlp_w1, mlp_w2):
    return _trunk(x, norm_mix_g, norm_mlp_g, final_g, a_w_in, a_v_g, a_w_s, a_b_s, a_w_out, b_w_in,
                  b_conv_w, b_w_out, c_w_in, c_w_grp, c_scale, d_w_in, d_kv_g, d_w_uv, d_w_out,
                  mlp_w1, mlp_w2)
```

```python
import functools

import jax
import jax.numpy as jnp
from jax import lax
from jax.experimental import pallas as pl
from jax.experimental.pallas import tpu as pltpu

EPS = 1e-6
LANES = 128
VMEM_LIMIT_BYTES = 56 * 1024 * 1024

A_CHUNK = 128
A_GROUPS = 8
B_KSIZE = 3
C_WINDOWS = (2, 4, 8, 16)
D_HEADS = 16
D_LAT = 128
D_VDIM = 64
D_IDX_HEADS = 8
D_IDX_DIM = 64
D_TOPK_MAX = 256

INT_MIN = -(2 ** 31)
NEG_BIAS = -1e30
M_INIT = -1e29
LOG2E = 1.4426950408889634

TOKEN_TILE = 1024
FUSED_TOKEN_TILE = 512
DSA_Q_TILE = 128
DSA_K_TILE = 512
DSA_SCORE_ROWS = 256
ATT_COLS = 256
QK_AHEAD = 5
SOFTMAX_ROWS = 128
SUM_ROWS = 16
SUBLANES = 8
WORD_BITS = 32

f32 = jnp.float32
bf16 = jnp.bfloat16
i32 = jnp.int32


def _dot(a, b):
    return jnp.dot(a, b, preferred_element_type=f32)


def _dot_nt(a, b):
    return lax.dot_general(a, b, (((1,), (1,)), ((), ())), preferred_element_type=f32)


def _rms(x, g):
    ms = jnp.mean(x * x, axis=-1, keepdims=True)
    return x * lax.rsqrt(ms + EPS) * g


def _params(*sem):
    return pltpu.CompilerParams(dimension_semantics=sem, vmem_limit_bytes=VMEM_LIMIT_BYTES)


def _full(shape):
    nd = len(shape)
    return pl.BlockSpec(shape, lambda *_: (0,) * nd)


def _mlp_body(x, g_ref, w1_ref, w2_ref, fg_ref, ff_chunk, apply_final):
    h = _rms(x, g_ref[...]).astype(bf16)
    acc = x
    for c in range(w1_ref.shape[1] // ff_chunk):
        a = jnp.maximum(_dot(h, w1_ref[:, c * ff_chunk:(c + 1) * ff_chunk]), 0.0)
        acc = acc + _dot((a * a).astype(bf16), w2_ref[c * ff_chunk:(c + 1) * ff_chunk, :])
    if apply_final:
        acc = _rms(acc, fg_ref[...])
    return acc


def _mlp_kernel(x_ref, g_ref, w1_ref, w2_ref, fg_ref, o_ref, *, ff_chunk, apply_final):
    o_ref[...] = _mlp_body(x_ref[...], g_ref, w1_ref, w2_ref, fg_ref, ff_chunk, apply_final)


def _mlp(x, g, w1, w2, final_g, apply_final):
    n, d = x.shape
    dff = w1.shape[1]
    tm = min(TOKEN_TILE, n)
    row = pl.BlockSpec((tm, d), lambda i: (i, 0))
    return pl.pallas_call(
        functools.partial(_mlp_kernel, ff_chunk=min(1024, dff), apply_final=apply_final),
        grid=(n // tm,),
        in_specs=[row, _full((1, d)), _full((d, dff)), _full((dff, d)), _full((1, d))],
        out_specs=row,
        out_shape=jax.ShapeDtypeStruct((n, d), f32),
        compiler_params=_params("parallel"),
    )(x, g.reshape(1, d), w1.astype(bf16), w2.astype(bf16), final_g.reshape(1, d))


def _sgu_kernel(x_ref, g_ref, win_ref, vg_ref, ws_ref, bst_ref, wout_ref, o_ref, s_ref):
    tm, _ = x_ref.shape
    width = wout_ref.shape[0]
    gdim = width // A_GROUPS
    x = x_ref[...]
    h = _rms(x, g_ref[...]).astype(bf16)
    u = jax.nn.gelu(_dot(h, win_ref[:, :width]))
    v = jax.nn.gelu(_dot(h, win_ref[:, width:]))
    v = _rms(v, vg_ref[...]).astype(bf16)
    r = lax.broadcasted_iota(i32, (A_CHUNK, A_CHUNK), 0)
    c = lax.broadcasted_iota(i32, (A_CHUNK, A_CHUNK), 1)
    causal = c <= r
    bst = bst_ref[...]
    for gi in range(A_GROUPS):
        ws = jnp.where(causal, ws_ref[gi], 0.0).astype(bf16)
        bias = jnp.broadcast_to(bst[:, gi:gi + 1], (A_CHUNK, gdim))
        for n in range(tm // A_CHUNK):
            vb = v[n * A_CHUNK:(n + 1) * A_CHUNK, gi * gdim:(gi + 1) * gdim]
            s_ref[n * A_CHUNK:(n + 1) * A_CHUNK, gi * gdim:(gi + 1) * gdim] = _dot(ws, vb) + bias
    o_ref[...] = x + _dot((u * s_ref[...]).astype(bf16), wout_ref[...])


def _mixer_sgu(x, g, w_in, v_g, w_s, b_s, w_out):
    n, d = x.shape
    width = w_out.shape[0]
    tm = min(TOKEN_TILE, n)
    row = pl.BlockSpec((tm, d), lambda i: (i, 0))
    return pl.pallas_call(
        _sgu_kernel,
        grid=(n // tm,),
        in_specs=[row, _full((1, d)), _full((d, 2 * width)), _full((1, width)),
                  _full(w_s.shape), _full((A_CHUNK, A_GROUPS)), _full((width, d))],
        out_specs=row,
        out_shape=jax.ShapeDtypeStruct((n, d), f32),
        scratch_shapes=[pltpu.VMEM((tm, width), f32)],
        compiler_params=_params("parallel"),
    )(x, g.reshape(1, d), w_in.astype(bf16), v_g.reshape(1, width), w_s, b_s.T, w_out.astype(bf16))


CONV_HALO = 8


def _conv_kernel(x_ref, g_ref, win_ref, cw_ref, wout_ref, o_ref, z_ref):
    tm = x_ref.shape[1]
    width = wout_ref.shape[0]

    @pl.when(pl.program_id(1) == 0)
    def _():
        z_ref[0:CONV_HALO, :] = jnp.zeros((CONV_HALO, width), f32)

    x = x_ref[0]
    h = _rms(x, g_ref[...]).astype(bf16)
    bg = _dot(h, win_ref[:, :width])
    z = _dot(h, win_ref[:, width:2 * width]) * _dot(h, win_ref[:, 2 * width:])
    z_ref[CONV_HALO:CONV_HALO + tm, :] = z
    y = cw_ref[B_KSIZE - 1:B_KSIZE, :] * z
    for k in range(B_KSIZE - 1):
        shift = B_KSIZE - 1 - k
        y = y + cw_ref[k:k + 1, :] * z_ref[CONV_HALO - shift:CONV_HALO - shift + tm, :]
    z_ref[0:CONV_HALO, :] = z[tm - CONV_HALO:, :]
    o_ref[0] = x + _dot((bg * y).astype(bf16), wout_ref[...])


def _mixer_conv(x, g, w_in, conv_w, w_out):
    b, l, d = x.shape
    width = w_out.shape[0]
    tm = min(TOKEN_TILE, l)
    row = pl.BlockSpec((1, tm, d), lambda i, j: (i, j, 0))
    return pl.pallas_call(
        _conv_kernel,
        grid=(b, l // tm),
        in_specs=[row, _full((1, d)), _full((d, 3 * width)), _full((B_KSIZE, width)), _full((width, d))],
        out_specs=row,
        out_shape=jax.ShapeDtypeStruct((b, l, d), f32),
        scratch_shapes=[pltpu.VMEM((CONV_HALO + tm, width), f32)],
        compiler_params=_params("arbitrary", "arbitrary"),
    )(x, g.reshape(1, d), w_in.astype(bf16), conv_w.reshape(B_KSIZE, width), w_out.astype(bf16))


POOL_HALO = 16
POOL_PAD = 8


def _pool_kernel(x_ref, g_ref, win_ref, wgrp_ref, sc_ref, o_ref, carry_ref, pa_ref, pb_ref, y_ref):
    tm = x_ref.shape[1]
    d = x_ref.shape[2]
    gdim = d // len(C_WINDOWS)
    n_ext = tm + POOL_HALO
    j = pl.program_id(1)

    @pl.when(j == 0)
    def _():
        carry_ref[...] = jnp.zeros(carry_ref.shape, f32)

    x = x_ref[0]
    h = _rms(x, g_ref[...]).astype(bf16)
    z = _dot(h, win_ref[...])
    pa_ref[0:POOL_PAD, :] = jnp.zeros((POOL_PAD, gdim), f32)
    pb_ref[0:POOL_PAD, :] = jnp.zeros((POOL_PAD, gdim), f32)
    pos = j * tm + lax.broadcasted_iota(i32, (tm, 1), 0) + 1
    for gi, w in enumerate(C_WINDOWS):
        zg = z[:, gi * gdim:(gi + 1) * gdim]
        src, dst = pa_ref, pb_ref
        src[POOL_PAD:POOL_PAD + POOL_HALO, :] = carry_ref[:, gi * gdim:(gi + 1) * gdim]
        src[POOL_PAD + POOL_HALO:, :] = zg
        shift = 1
        while True:
            acc = src[POOL_PAD:, :] + src[POOL_PAD - shift:POOL_PAD - shift + n_ext, :]
            shift *= 2
            if shift == w:
                break
            dst[POOL_PAD:, :] = acc
            src, dst = dst, src
        cnt = jnp.minimum(pos, w).astype(f32)
        pooled = acc[POOL_HALO:, :] / cnt - zg
        y_ref[:, gi * gdim:(gi + 1) * gdim] = _dot(pooled.astype(bf16), wgrp_ref[gi])
    carry_ref[...] = z[tm - POOL_HALO:, :]
    o_ref[0] = x + y_ref[...] * sc_ref[...]


def _mixer_pool(x, g, w_in, w_grp, scale):
    b, l, d = x.shape
    gdim = d // len(C_WINDOWS)
    tm = min(TOKEN_TILE, l)
    row = pl.BlockSpec((1, tm, d), lambda i, j: (i, j, 0))
    ext = POOL_PAD + POOL_HALO + tm
    return pl.pallas_call(
        _pool_kernel,
        grid=(b, l // tm),
        in_specs=[row, _full((1, d)), _full((d, d)), _full(w_grp.shape), _full((1, d))],
        out_specs=row,
        out_shape=jax.ShapeDtypeStruct((b, l, d), f32),
        scratch_shapes=[pltpu.VMEM((POOL_HALO, d), f32), pltpu.VMEM((ext, gdim), f32),
                        pltpu.VMEM((ext, gdim), f32), pltpu.VMEM((tm, d), f32)],
        compiler_params=_params("arbitrary", "arbitrary"),
    )(x, g.reshape(1, d), w_in.astype(bf16), w_grp.astype(bf16), scale.reshape(1, d))


def _dsa_proj_kernel(x_ref, g_ref, wq_ref, wckw_ref, wqi_ref, kvg_ref,
                     q_ref, c_ref, ct_ref, qi_ref, ki_ref, wit_ref):
    tm = x_ref.shape[1]
    h = _rms(x_ref[0], g_ref[...]).astype(bf16)
    q = _dot(h, wq_ref[...])
    for hd in range(D_HEADS):
        q_ref[0, hd] = q[:, hd * D_LAT:(hd + 1) * D_LAT].astype(bf16)
    ckw = _dot(h, wckw_ref[...])
    c = _rms(ckw[:, :D_LAT], kvg_ref[...])
    kw = ckw[:, D_LAT:]
    c_ref[0] = c.astype(bf16)
    for blk in range(tm // DSA_SCORE_ROWS):
        rows = slice(blk * DSA_SCORE_ROWS, (blk + 1) * DSA_SCORE_ROWS)
        ct_ref[0, blk] = c[rows, :].T.astype(bf16)
    qi = _dot(h, wqi_ref[...])
    for hd in range(D_IDX_HEADS):
        qi_ref[0, hd] = qi[:, hd * D_IDX_DIM:(hd + 1) * D_IDX_DIM].astype(bf16)
    ki_ref[0] = kw[:, :D_IDX_DIM].astype(bf16)
    wit_ref[0] = kw.T[D_IDX_DIM:D_IDX_DIM + D_IDX_HEADS, :] * (D_IDX_HEADS ** -0.5 * D_IDX_DIM ** -0.5)


def _dsa_proj(x, g, w_in, kv_g):
    b, l, d = x.shape
    nq = D_HEADS * D_LAT
    nqi = D_IDX_HEADS * D_IDX_DIM
    o_c, o_qi, o_ki = nq, nq + D_LAT, nq + D_LAT + nqi
    w_in = w_in.astype(bf16)
    w_ckw = jnp.concatenate(
        [w_in[:, o_c:o_qi], jnp.pad(w_in[:, o_ki:], ((0, 0), (0, LANES - (D_IDX_DIM + D_IDX_HEADS))))], axis=1)
    tm = min(TOKEN_TILE, l)
    row = pl.BlockSpec((1, tm, d), lambda i, j: (i, j, 0))
    return pl.pallas_call(
        _dsa_proj_kernel,
        grid=(b, l // tm),
        in_specs=[row, _full((1, d)), _full((d, nq)), _full((d, D_LAT + LANES)), _full((d, nqi)),
                  _full((1, D_LAT))],
        out_specs=[pl.BlockSpec((1, D_HEADS, tm, D_LAT), lambda i, j: (i, 0, j, 0)),
                   pl.BlockSpec((1, tm, D_LAT), lambda i, j: (i, j, 0)),
                   pl.BlockSpec((1, tm // DSA_SCORE_ROWS, D_LAT, DSA_SCORE_ROWS), lambda i, j: (i, j, 0, 0)),
                   pl.BlockSpec((1, D_IDX_HEADS, tm, D_IDX_DIM), lambda i, j: (i, 0, j, 0)),
                   pl.BlockSpec((1, tm, D_IDX_DIM), lambda i, j: (i, j, 0)),
                   pl.BlockSpec((1, D_IDX_HEADS, tm), lambda i, j: (i, 0, j))],
        out_shape=[jax.ShapeDtypeStruct((b, D_HEADS, l, D_LAT), bf16),
                   jax.ShapeDtypeStruct((b, l, D_LAT), bf16),
                   jax.ShapeDtypeStruct((b, l // DSA_SCORE_ROWS, D_LAT, DSA_SCORE_ROWS), bf16),
                   jax.ShapeDtypeStruct((b, D_IDX_HEADS, l, D_IDX_DIM), bf16),
                   jax.ShapeDtypeStruct((b, l, D_IDX_DIM), bf16),
                   jax.ShapeDtypeStruct((b, D_IDX_HEADS, l), f32)],
        compiler_params=_params("parallel", "parallel"),
    )(x, g.reshape(1, d), w_in[:, :o_c], w_ckw, w_in[:, o_qi:o_ki], kv_g.reshape(1, D_LAT))


def _bit_transpose(words):
    a = list(reversed(words))
    j, mask = WORD_BITS // 2, 0x0000FFFF
    while j:
        k = 0
        while k < WORD_BITS:
            t = (a[k] ^ lax.shift_right_logical(a[k + j], jnp.int32(j))) & mask
            a[k] = a[k] ^ t
            a[k + j] = a[k + j] ^ (t << j)
            k = (k + j + 1) & ~j
        j >>= 1
        mask ^= mask << j
    return a


def _dsa_attn_kernel(q_ref, qi_ref, wit_ref, c_ref, ct_ref, ki_ref, o_ref,
                     key_ref, plane_ref, qx_ref, lg_ref, m_ref, acc_ref, *, topk):
    tq = q_ref.shape[2]
    tk = DSA_K_TILE
    ts = DSA_SCORE_ROWS
    nrow = D_HEADS * tq
    n_word = plane_ref.shape[1]
    t0 = pl.program_id(1) * tq
    n_blk = (t0 + tq + ts - 1) // ts
    kf = float(topk)

    @pl.when((pl.program_id(0) == 0) & (pl.program_id(1) == 0))
    def _():
        plane_ref[...] = jnp.zeros(plane_ref.shape, i32)

    qi = qi_ref[0].reshape(D_IDX_HEADS * tq, D_IDX_DIM)
    wit = wit_ref[0]
    key_pos = lax.broadcasted_iota(i32, (ts, tq), 0)
    qry_pos = t0 + lax.broadcasted_iota(i32, (ts, tq), 1)

    def score_block(b):
        k0 = pl.multiple_of(b * ts, ts)
        lg = _dot_nt(ki_ref[0, pl.ds(k0, ts), :], qi)
        s = None
        for hd in range(D_IDX_HEADS):
            term = jnp.maximum(lg[:, hd * tq:(hd + 1) * tq], 0.0) * wit[hd:hd + 1, :]
            s = term if s is None else s + term
        s = jnp.where(s == 0.0, 0.0, s)
        bits = pltpu.bitcast(s, i32)
        key = bits ^ ((bits >> 31) & 0x7FFFFFFF)
        key = jnp.where(k0 + key_pos <= qry_pos, key, INT_MIN)
        key_ref[pl.ds(k0, ts), :] = key
        unsigned = (key ^ INT_MIN).reshape(WORD_BITS, SUBLANES, tq)
        planes = _bit_transpose([unsigned[m] for m in range(WORD_BITS)])
        w0 = pl.multiple_of(b * SUBLANES, SUBLANES)
        for p in range(WORD_BITS):
            plane_ref[p, pl.ds(w0, SUBLANES), :] = planes[p]

    def score_pair(j, carry):
        score_block(2 * j)
        score_block(2 * j + 1)
        return carry

    lax.fori_loop(0, n_blk // 2, score_pair, 0)

    @pl.when(n_blk % 2 == 1)
    def _():
        score_block(n_blk - 1)

    word = lax.broadcasted_iota(i32, (n_word, tq), 0)
    first_row = (word >> 3) * ts + (word & (SUBLANES - 1))
    n_valid = ((t0 + lax.broadcasted_iota(i32, (n_word, tq), 1) - first_row) >> 3) + 1
    cand0 = jnp.where(n_valid >= WORD_BITS, -1,
                      lax.shift_left(jnp.int32(1), jnp.clip(n_valid, 0, WORD_BITS - 1)) - 1)

    def row_sum(v):
        parts = [v[r:r + SUBLANES] for r in range(0, n_word, SUBLANES)]
        while len(parts) > 1:
            parts = [a + b for a, b in zip(parts[::2], parts[1::2])] + parts[len(parts) & ~1:]
        return jnp.sum(parts[0].astype(f32), axis=0, keepdims=True)

    def plane_body(p, carry):
        cand, need, thr = carry
        ones = cand & plane_ref[p]
        n_ones = row_sum(lax.population_count(ones))
        take = n_ones >= need
        thr = jnp.where(take, thr | lax.shift_left(jnp.int32(1), WORD_BITS - 1 - p), thr)
        return jnp.where(take, ones, cand ^ ones), jnp.where(take, need, need - n_ones), thr

    cand, keep_n, thr = lax.fori_loop(
        0, WORD_BITS, plane_body, (cand0, jnp.full((1, tq), kf, f32), jnp.zeros((1, tq), i32)))
    n_tied = row_sum(lax.population_count(cand))
    few = t0 + lax.broadcasted_iota(i32, (1, tq), 1) + 1 < topk
    thr_row = jnp.where(few, INT_MIN + 1, thr ^ INT_MIN)

    @pl.when(jnp.max(jnp.where(few, 0.0, n_tied - keep_n)) > 0.0)
    def _():
        tri = (lax.broadcasted_iota(i32, (LANES, LANES), 1)
               <= lax.broadcasted_iota(i32, (LANES, LANES), 0)).astype(bf16)

        def tie_body(j, run):
            k0 = pl.multiple_of(j * LANES, LANES)
            blk = key_ref[pl.ds(k0, LANES), :]
            tie = jnp.where(blk == thr_row, 1.0, 0.0)
            rank = run + _dot(tri, tie.astype(bf16))
            drop = (tie * jnp.where(rank > keep_n, 1.0, 0.0)) > 0.5
            key_ref[pl.ds(k0, LANES), :] = jnp.where(drop, INT_MIN, blk)
            return run + jnp.sum(tie, axis=0, keepdims=True)

        lax.fori_loop(0, n_blk * (ts // LANES), tie_body, jnp.zeros((1, tq), f32))

    n_cb = nrow // ATT_COLS
    heads_per_cb = ATT_COLS // tq
    m_ref[...] = jnp.full(m_ref.shape, M_INIT, f32)
    acc_ref[...] = jnp.zeros(acc_ref.shape, f32)
    onehot = (lax.broadcasted_iota(i32, (tq, ATT_COLS), 0)
              == (lax.broadcasted_iota(i32, (tq, ATT_COLS), 1) & (tq - 1)))
    onehot = jnp.where(onehot, 1.0, 0.0).astype(bf16)
    for cb in range(n_cb):
        for i in range(heads_per_cb):
            qx_ref[cb, :D_LAT, i * tq:(i + 1) * tq] = q_ref[0, cb * heads_per_cb + i].astype(f32).T.astype(bf16)
        qx_ref[cb, D_LAT:, :] = onehot
    scale = D_LAT ** -0.5 * LOG2E

    def col_max(v):
        parts = [v[r:r + SUBLANES] for r in range(0, v.shape[0], SUBLANES)]
        while len(parts) > 1:
            parts = [jnp.maximum(a, b) for a, b in zip(parts[::2], parts[1::2])] + parts[len(parts) & ~1:]
        return jnp.max(parts[0], axis=0, keepdims=True)

    def att_tile(blk0, n_sub):
        width = n_sub * ts
        k0 = pl.multiple_of(blk0 * ts, ts)
        mask_t = jnp.where(key_ref[pl.ds(k0, width), :] >= thr_row, 0.0, NEG_BIAS).astype(bf16)
        keys = jnp.concatenate([c_ref[0, pl.ds(k0, width), :], mask_t], axis=1)
        vals_t = jnp.concatenate([ct_ref[0, blk0 + i] for i in range(n_sub)], axis=1)
        vals_t = jnp.concatenate([vals_t, jnp.ones((SUM_ROWS, width), bf16)], axis=0)
        for cb in range(min(QK_AHEAD, n_cb)):
            lg_ref[cb, :width, :] = _dot(keys, qx_ref[cb])
        for cb in range(n_cb):
            if cb + QK_AHEAD < n_cb:
                lg_ref[cb + QK_AHEAD, :width, :] = _dot(keys, qx_ref[cb + QK_AHEAD])
            m_prev = m_ref[cb]
            m_new = m_prev
            for r0 in range(0, width, SOFTMAX_ROWS):
                m_new = jnp.maximum(m_new, col_max(lg_ref[cb, r0:r0 + SOFTMAX_ROWS, :]))
            alpha = jnp.exp2((m_prev - m_new) * scale)
            p = jnp.concatenate(
                [jnp.exp2((lg_ref[cb, r0:r0 + SOFTMAX_ROWS, :] - m_new) * scale).astype(bf16)
                 for r0 in range(0, width, SOFTMAX_ROWS)], axis=0)
            m_ref[cb] = m_new
            acc_ref[cb] = acc_ref[cb] * alpha + _dot(vals_t, p)

    def att_body(j, carry):
        att_tile(j * (tk // ts), tk // ts)
        return carry

    lax.fori_loop(0, n_blk // (tk // ts), att_body, 0)

    @pl.when(n_blk % (tk // ts) == 1)
    def _():
        att_tile(n_blk - 1, 1)

    for cb in range(n_cb):
        out_t = acc_ref[cb, :D_LAT, :] / acc_ref[cb, D_LAT:D_LAT + 1, :]
        for i in range(heads_per_cb):
            hd = cb * heads_per_cb + i
            o_ref[0, :, hd * D_LAT:(hd + 1) * D_LAT] = out_t[:, i * tq:(i + 1) * tq].T.astype(bf16)


def _dsa_attn(q, qi, wit, c, ct, ki, topk):
    b, _, l, _ = q.shape
    tq = DSA_Q_TILE
    assert tq == LANES and l % DSA_K_TILE == 0 and DSA_K_TILE == 2 * DSA_SCORE_ROWS
    assert DSA_SCORE_ROWS == WORD_BITS * SUBLANES
    n_cb = D_HEADS * tq // ATT_COLS
    return pl.pallas_call(
        functools.partial(_dsa_attn_kernel, topk=topk),
        grid=(b, l // tq),
        in_specs=[pl.BlockSpec((1, D_HEADS, tq, D_LAT), lambda i, j: (i, 0, j, 0)),
                  pl.BlockSpec((1, D_IDX_HEADS, tq, D_IDX_DIM), lambda i, j: (i, 0, j, 0)),
                  pl.BlockSpec((1, D_IDX_HEADS, tq), lambda i, j: (i, 0, j)),
                  pl.BlockSpec((1, l, D_LAT), lambda i, j: (i, 0, 0)),
                  pl.BlockSpec((1, l // DSA_SCORE_ROWS, D_LAT, DSA_SCORE_ROWS), lambda i, j: (i, 0, 0, 0)),
                  pl.BlockSpec((1, l, D_IDX_DIM), lambda i, j: (i, 0, 0))],
        out_specs=pl.BlockSpec((1, tq, D_HEADS * D_LAT), lambda i, j: (i, j, 0)),
        out_shape=jax.ShapeDtypeStruct((b, l, D_HEADS * D_LAT), bf16),
        scratch_shapes=[pltpu.VMEM((l, tq), i32),
                        pltpu.VMEM((WORD_BITS, l // WORD_BITS, tq), i32),
                        pltpu.VMEM((n_cb, D_LAT + tq, ATT_COLS), bf16),
                        pltpu.VMEM((n_cb, DSA_K_TILE, ATT_COLS), f32),
                        pltpu.VMEM((n_cb, 1, ATT_COLS), f32),
                        pltpu.VMEM((n_cb, D_LAT + SUM_ROWS, ATT_COLS), f32)],
        compiler_params=_params("arbitrary", "arbitrary"),
    )(q, qi, wit, c, ct, ki)


def _dsa_out_mlp_kernel(x_ref, o_ref, wuv_ref, wout_ref, g_ref, w1_ref, w2_ref, fg_ref, y_ref, ov_ref,
                        *, ff_chunk, apply_final):
    for hd in range(D_HEADS):
        ov_ref[:, hd * D_VDIM:(hd + 1) * D_VDIM] = _dot(o_ref[:, hd * D_LAT:(hd + 1) * D_LAT], wuv_ref[hd])
    x = x_ref[...] + _dot(ov_ref[...].astype(bf16), wout_ref[...])
    y_ref[...] = _mlp_body(x, g_ref, w1_ref, w2_ref, fg_ref, ff_chunk, apply_final)


def _dsa_out_mlp(x, o, w_uv, w_out, g, w1, w2, final_g, apply_final):
    n, d = x.shape
    dff = w1.shape[1]
    tm = min(FUSED_TOKEN_TILE, n)
    row = pl.BlockSpec((tm, d), lambda i: (i, 0))
    return pl.pallas_call(
        functools.partial(_dsa_out_mlp_kernel, ff_chunk=min(1024, dff), apply_final=apply_final),
        grid=(n // tm,),
        in_specs=[row, pl.BlockSpec((tm, D_HEADS * D_LAT), lambda i: (i, 0)),
                  _full(w_uv.shape), _full(w_out.shape),
                  _full((1, d)), _full((d, dff)), _full((dff, d)), _full((1, d))],
        out_specs=row,
        out_shape=jax.ShapeDtypeStruct((n, d), f32),
        scratch_shapes=[pltpu.VMEM((tm, D_HEADS * D_VDIM), f32)],
        compiler_params=_params("parallel"),
    )(x, o, w_uv.astype(bf16), w_out.astype(bf16), g.reshape(1, d), w1.astype(bf16), w2.astype(bf16),
      final_g.reshape(1, d))


def _mixer_dsa_mlp(x, g, w_in, kv_g, w_uv, w_out, mlp_g, w1, w2, final_g, apply_final):
    b, l, d = x.shape
    topk = min(D_TOPK_MAX, l // 4)
    q, c, ct, qi, ki, wit = _dsa_proj(x, g, w_in, kv_g)
    o = _dsa_attn(q, qi, wit, c, ct, ki, topk)
    return _dsa_out_mlp(x.reshape(b * l, d), o.reshape(b * l, D_HEADS * D_LAT), w_uv, w_out,
                        mlp_g, w1, w2, final_g, apply_final).reshape(b, l, d)


@jax.jit
def _trunk(x, norm_mix_g, norm_mlp_g, final_g, a_w_in, a_v_g, a_w_s, a_b_s, a_w_out, b_w_in, b_conv_w,
           b_w_out, c_w_in, c_w_grp, c_scale, d_w_in, d_kv_g, d_w_uv, d_w_out, mlp_w1, mlp_w2):
    b, l, d = x.shape
    depth = norm_mix_g.shape[0]
    n_mixers = 4
    for i in range(depth):
        m, j = i % n_mixers, i // n_mixers
        g = norm_mix_g[i]
        if m == 0:
            x = _mixer_sgu(x.reshape(b * l, d), g, a_w_in[j], a_v_g[j], a_w_s[j], a_b_s[j],
                           a_w_out[j]).reshape(b, l, d)
        elif m == 1:
            x = _mixer_conv(x, g, b_w_in[j], b_conv_w[j], b_w_out[j])
        elif m == 2:
            x = _mixer_pool(x, g, c_w_in[j], c_w_grp[j], c_scale[j])
        else:
            x = _mixer_dsa_mlp(x, g, d_w_in[j], d_kv_g[j], d_w_uv[j], d_w_out[j], norm_mlp_g[i],
                               mlp_w1[i], mlp_w2[i], final_g, apply_final=(i == depth - 1))
            continue
        x = _mlp(x.reshape(b * l, d), norm_mlp_g[i], mlp_w1[i], mlp_w2[i], final_g,
                 apply_final=(i == depth - 1)).reshape(b, l, d)
    return x


def kernel(x, norm_mix_g, norm_mlp_g, final_g, a_w_in, a_v_g, a_w_s, a_b_s, a_w_out, b_w_in, b_conv_w,
           b_w_out, c_w_in, c_w_grp, c_scale, d_w_in, d_kv_g, d_w_uv, d_w_out, mlp_w1, mlp_w2):
    return _trunk(x, norm_mix_g, norm_mlp_g, final_g, a_w_in, a_v_g, a_w_s, a_b_s, a_w_out, b_w_in,
                  b_conv_w, b_w_out, c_w_in, c_w_grp, c_scale, d_w_in, d_kv_g, d_w_uv, d_w_out,
                  mlp_w1, mlp_w2)
```
